```python
import math
import jax, jax.numpy as jnp
from jax import lax
import numpy as np

D_MODEL = 1024
BATCH = 4
SEQ = 8192
DEPTH = 2

HEAD_DIM = 64
N_HEADS = D_MODEL // HEAD_DIM
HEADS_A = N_HEADS // 2
HEADS_B = N_HEADS - HEADS_A
KV_A = 2
KV_B = 2
G_A = HEADS_A // KV_A
G_B = HEADS_B // KV_B
Q_A = HEADS_A * HEAD_DIM
KVD_A = KV_A * HEAD_DIM
Q_B = HEADS_B * HEAD_DIM
KVD_B = KV_B * HEAD_DIM
IN_COLS = Q_A + 2 * KVD_A + Q_B + 2 * KVD_B
IN_SPLITS = (Q_A, Q_A + KVD_A, Q_A + 2 * KVD_A, Q_A + 2 * KVD_A + Q_B, Q_A + 2 * KVD_A + Q_B + KVD_B)
MIX_WIDTH = Q_A + Q_B

GRID_W = 64
ROPE_THETA = 10000.0
Q_BLOCK = 128
WINDOW = 128
N_BUCKETS = 32
MAX_DISTANCE = 128
D_FF = 2816
CONV_W = 3
ALPHA = (2.0 * DEPTH) ** 0.25
BETA = (8.0 * DEPTH) ** -0.25
RMS_EPS = 1e-6
LN_EPS = 1e-5

kernel_name = "hymba_axial_swa_convglu_deepnorm_encoder"


def rms_norm(x, g):
    xf = x.astype(jnp.float32)
    y = xf * lax.rsqrt(jnp.mean(xf * xf, axis=-1, keepdims=True) + RMS_EPS)
    return (y * g.astype(jnp.float32)).astype(x.dtype)


def layer_norm(x, g, b):
    xf = x.astype(jnp.float32)
    mu = jnp.mean(xf, axis=-1, keepdims=True)
    var = jnp.mean(jnp.square(xf - mu), axis=-1, keepdims=True)
    y = (xf - mu) * lax.rsqrt(var + LN_EPS)
    return (y * g.astype(jnp.float32) + b.astype(jnp.float32)).astype(x.dtype)


def axial_rope_tables(seq_len):
    rows_n = seq_len // GRID_W
    row = jnp.repeat(jnp.arange(rows_n, dtype=jnp.float32), GRID_W)
    col = jnp.tile(jnp.arange(GRID_W, dtype=jnp.float32), rows_n)
    half = HEAD_DIM // 2
    inv_freq = ROPE_THETA ** (-jnp.arange(0, half, 2, dtype=jnp.float32) / half)
    ang = jnp.concatenate([row[:, None] * inv_freq, col[:, None] * inv_freq], axis=-1)
    return jnp.cos(ang), jnp.sin(ang)


def apply_rope(x, cos, sin):
    xf = x.astype(jnp.float32).reshape(x.shape[:-1] + (HEAD_DIM // 2, 2))
    x0, x1 = xf[..., 0], xf[..., 1]
    c = cos[None, :, None, :]
    s = sin[None, :, None, :]
    out = jnp.stack([x0 * c - x1 * s, x0 * s + x1 * c], axis=-1).reshape(x.shape)
    return out.astype(x.dtype)


def global_gqa(q, k, v):
    B, S = q.shape[0], q.shape[1]
    nb = S // Q_BLOCK
    qb = q.reshape(B, nb, Q_BLOCK, KV_A, G_A, HEAD_DIM).transpose(1, 0, 2, 3, 4, 5)
    scale = HEAD_DIM ** -0.5

    def block(qi):
        s = jnp.einsum('bqhgd,bkhd->bhgqk', qi, k, preferred_element_type=jnp.float32) * scale
        p = jax.nn.softmax(s, axis=-1)
        return jnp.einsum('bhgqk,bkhd->bqhgd', p.astype(v.dtype), v)

    o = lax.map(block, qb)
    return o.transpose(1, 0, 2, 3, 4, 5).reshape(B, S, Q_A)


def t5_bucket(rel):
    half = N_BUCKETS // 2
    max_exact = half // 2
    bucket = jnp.where(rel > 0, half, 0)
    rp = jnp.abs(rel)
    rpf = jnp.maximum(rp, 1).astype(jnp.float32)
    large = max_exact + (jnp.log(rpf / max_exact) / math.log(MAX_DISTANCE / max_exact)
                         * (half - max_exact)).astype(jnp.int32)
    large = jnp.minimum(large, half - 1)
    return bucket + jnp.where(rp < max_exact, rp, large)


def window_gqa_sink(q, k, v, rel_bias, sink):
    B, S = q.shape[0], q.shape[1]
    nb = S // Q_BLOCK
    scale = HEAD_DIM ** -0.5
    qb = q.reshape(B, nb, Q_BLOCK, KV_B, G_B, HEAD_DIM)
    pad = ((0, 0), (Q_BLOCK, Q_BLOCK), (0, 0), (0, 0))
    kp = jnp.pad(k, pad).reshape(B, nb + 2, Q_BLOCK, KV_B, HEAD_DIM)
    vp = jnp.pad(v, pad).reshape(B, nb + 2, Q_BLOCK, KV_B, HEAD_DIM)
    kb = jnp.concatenate([kp[:, :-2], kp[:, 1:-1], kp[:, 2:]], axis=2)
    vb = jnp.concatenate([vp[:, :-2], vp[:, 1:-1], vp[:, 2:]], axis=2)
    qpos = jnp.arange(Q_BLOCK, dtype=jnp.int32)
    kpos = jnp.arange(3 * Q_BLOCK, dtype=jnp.int32) - Q_BLOCK
    rel = kpos[None, :] - qpos[:, None]
    bias = rel_bias.astype(jnp.float32)[t5_bucket(rel)]
    bias = bias.transpose(2, 0, 1).reshape(KV_B, G_B, Q_BLOCK, 3 * Q_BLOCK)
    kabs = jnp.arange(nb, dtype=jnp.int32)[:, None] * Q_BLOCK + kpos[None, :]
    valid = (jnp.abs(rel) <= WINDOW)[None] & ((kabs >= 0) & (kabs < S))[:, None, :]
    s = jnp.einsum('bnqhgd,bnkhd->bnhgqk', qb, kb, preferred_element_type=jnp.float32) * scale + bias
    s = jnp.where(valid[None, :, None, None], s, -jnp.inf)
    sink_logit = jnp.broadcast_to(sink.astype(jnp.float32).reshape(KV_B, G_B)[None, None, :, :, None, None],
                                  s.shape[:-1] + (1,))
    p = jax.nn.softmax(jnp.concatenate([s, sink_logit], axis=-1), axis=-1)[..., :-1]
    o = jnp.einsum('bnhgqk,bnkhd->bnqhgd', p.astype(v.dtype), vb)
    return o.reshape(B, S, Q_B)


def token_mixer(x, rel_bias, cos, sin, w_in, q_norm, k_norm, sink, out_norm_a, out_norm_b, w_out):
    B, S, _ = x.shape
    h = jnp.einsum('bsd,de->bse', x, w_in)
    qa, ka, va, qb, kb, vb = jnp.split(h, IN_SPLITS, axis=-1)
    qa = apply_rope(rms_norm(qa.reshape(B, S, HEADS_A, HEAD_DIM), q_norm), cos, sin)
    ka = apply_rope(rms_norm(ka.reshape(B, S, KV_A, HEAD_DIM), k_norm), cos, sin)
    va = va.reshape(B, S, KV_A, HEAD_DIM)
    ya = rms_norm(global_gqa(qa, ka, va), out_norm_a)
    yb = window_gqa_sink(qb.reshape(B, S, HEADS_B, HEAD_DIM), kb.reshape(B, S, KV_B, HEAD_DIM),
                         vb.reshape(B, S, KV_B, HEAD_DIM), rel_bias, sink)
    yb = rms_norm(yb, out_norm_b)
    return jnp.einsum('bse,ed->bsd', jnp.concatenate([ya, yb], axis=-1), w_out)


def conv_glu(x, w_gate, w_up, conv_w, conv_b, w_down):
    S = x.shape[1]
    g = jnp.einsum('bsd,df->bsf', x, w_gate)
    u = jnp.einsum('bsd,df->bsf', x, w_up)
    r = CONV_W // 2
    gp = jnp.pad(g, ((0, 0), (r, r), (0, 0)))
    gc = conv_b
    for j in range(CONV_W):
        gc = gc + gp[:, j:j + S] * conv_w[j]
    return jnp.einsum('bsf,fd->bsd', jax.nn.gelu(gc) * u, w_down)


def setup_inputs(seed: int = 0) -> dict:
    key = jax.random.key(seed)
    ks = jax.random.split(key, 20)
    f32 = jnp.float32
    L = DEPTH

    def nrm(k, shape, scale):
        return jax.random.normal(k, shape, f32) * scale

    return {
        "x": nrm(ks[0], (BATCH, SEQ, D_MODEL), 1.0),
        "rel_bias": nrm(ks[1], (N_BUCKETS, HEADS_B), 0.1),
        "w_in": nrm(ks[2], (L, D_MODEL, IN_COLS), D_MODEL ** -0.5),
        "q_norm": 1.0 + nrm(ks[3], (L, HEAD_DIM), 0.05),
        "k_norm": 1.0 + nrm(ks[4], (L, HEAD_DIM), 0.05),
        "sink": nrm(ks[5], (L, HEADS_B), 0.5),
        "out_norm_a": 1.0 + nrm(ks[6], (L, Q_A), 0.05),
        "out_norm_b": 1.0 + nrm(ks[7], (L, Q_B), 0.05),
        "w_out": nrm(ks[8], (L, MIX_WIDTH, D_MODEL), BETA * MIX_WIDTH ** -0.5),
        "ln1_g": 1.0 + nrm(ks[9], (L, D_MODEL), 0.05),
        "ln1_b": nrm(ks[10], (L, D_MODEL), 0.01),
        "w_gate": nrm(ks[11], (L, D_MODEL, D_FF), D_MODEL ** -0.5),
        "w_up": nrm(ks[12], (L, D_MODEL, D_FF), D_MODEL ** -0.5),
        "conv_w": nrm(ks[13], (L, CONV_W, D_FF), CONV_W ** -0.5),
        "conv_b": nrm(ks[14], (L, D_FF), 0.01),
        "w_down": nrm(ks[15], (L, D_FF, D_MODEL), BETA * D_FF ** -0.5),
        "ln2_g": 1.0 + nrm(ks[16], (L, D_MODEL), 0.05),
        "ln2_b": nrm(ks[17], (L, D_MODEL), 0.01),
    }


def reference(x, rel_bias, w_in, q_norm, k_norm, sink, out_norm_a, out_norm_b, w_out,
              ln1_g, ln1_b, w_gate, w_up, conv_w, conv_b, w_down, ln2_g, ln2_b):
    cos, sin = axial_rope_tables(x.shape[1])
    for l in range(DEPTH):
        mix = token_mixer(x, rel_bias, cos, sin, w_in[l], q_norm[l], k_norm[l], sink[l],
                          out_norm_a[l], out_norm_b[l], w_out[l])
        x = layer_norm(ALPHA * x + mix, ln1_g[l], ln1_b[l])
        ffn = conv_glu(x, w_gate[l], w_up[l], conv_w[l], conv_b[l], w_down[l])
        x = layer_norm(ALPHA * x + ffn, ln2_g[l], ln2_b[l])
    return x
```

```python
import functools
import math

import jax
import jax.numpy as jnp
import numpy as np
from jax import lax
from jax.experimental import pallas as pl
from jax.experimental.pallas import tpu as pltpu

F32 = jnp.float32
BF16 = jnp.bfloat16

HEAD_DIM = 64
HEADS_A = 8
HEADS_B = 8
KV_A = 2
KV_B = 2
G_A = HEADS_A // KV_A
G_B = HEADS_B // KV_B
Q_A = HEADS_A * HEAD_DIM
KVD_A = KV_A * HEAD_DIM
Q_B = HEADS_B * HEAD_DIM
KVD_B = KV_B * HEAD_DIM
GRID_W = 64
ROPE_THETA = 10000.0
Q_BLOCK = 128
WINDOW = 128
N_BUCKETS = 32
MAX_DISTANCE = 128
CONV_W = 3
RMS_EPS = 1e-6
LN_EPS = 1e-5
LOG2E = math.log2(math.e)
SM_SCALE = HEAD_DIM ** -0.5

LANES = 128
V_ROWS = 80
VMEM_LIMIT = 56 * 1024 * 1024


def _cparams(sem):
    return pltpu.CompilerParams(dimension_semantics=sem, vmem_limit_bytes=VMEM_LIMIT)


def _resident(shape):
    nd = len(shape)
    return pl.BlockSpec(shape, lambda *_: (0,) * nd, pipeline_mode=pl.Buffered(1))


def _inproj_kernel(x_ref, w_ref, gq_ref, gk_ref, cos_ref, sin_ref, j_ref,
                   qat_ref, ka_ref, vat_ref, qb_ref, kb_ref, vb_ref):
    tm = x_ref.shape[1]
    x = x_ref[0].astype(BF16)
    h = jnp.dot(x, w_ref[...], preferred_element_type=F32)

    lane = lax.broadcasted_iota(jnp.int32, (tm, LANES), 1)
    first_half = (lane % HEAD_DIM) < (HEAD_DIM // 2)
    cosv = cos_ref[...]
    sinv = sin_ref[...]
    ones_blk = j_ref[...]

    def norm_rope(hc, gain):
        ssq = jnp.dot((hc * hc).astype(BF16), ones_blk, preferred_element_type=F32)
        y = hc * lax.rsqrt(ssq * (1.0 / HEAD_DIM) + RMS_EPS) * gain
        partner = jnp.where(first_half, pltpu.roll(y, LANES - HEAD_DIM // 2, 1),
                            pltpu.roll(y, HEAD_DIM // 2, 1))
        return y * cosv + partner * sinv

    q = jnp.concatenate(
        [norm_rope(h[:, c * LANES:(c + 1) * LANES], gq_ref[:, c * LANES:(c + 1) * LANES])
         for c in range(Q_A // LANES)], axis=1)
    qt = q.T
    for hh in range(HEADS_A):
        qat_ref[0, hh] = qt[hh * HEAD_DIM:(hh + 1) * HEAD_DIM].astype(BF16)

    k = norm_rope(h[:, Q_A:Q_A + KVD_A], gk_ref[...])
    v = h[:, Q_A + KVD_A:Q_A + 2 * KVD_A]
    vt = v.T
    row = lax.broadcasted_iota(jnp.int32, (V_ROWS - HEAD_DIM, tm), 0)
    ones_rows = jnp.where(row == 0, 1.0, 0.0).astype(BF16)
    for g in range(KV_A):
        ka_ref[0, g] = k[:, g * HEAD_DIM:(g + 1) * HEAD_DIM].astype(BF16)
        vat_ref[0, g, 0:HEAD_DIM, :] = vt[g * HEAD_DIM:(g + 1) * HEAD_DIM].astype(BF16)
        vat_ref[0, g, HEAD_DIM:V_ROWS, :] = ones_rows

    b0 = Q_A + 2 * KVD_A
    qb_ref[0] = (h[:, b0:b0 + Q_B] * (SM_SCALE * LOG2E)).astype(BF16)
    kb_ref[0] = h[:, b0 + Q_B:b0 + Q_B + KVD_B].astype(BF16)
    vb_ref[0] = h[:, b0 + Q_B + KVD_B:b0 + Q_B + 2 * KVD_B].astype(BF16)


def _inproj(x, w, gq, gk, cos_t, sin_t, ones_blk, *, tm):
    B, S, D = x.shape
    ns = S // tm
    n_cols = w.shape[1]
    out_shape = (
        jax.ShapeDtypeStruct((B, HEADS_A, HEAD_DIM, S), BF16),
        jax.ShapeDtypeStruct((B, KV_A, S, HEAD_DIM), BF16),
        jax.ShapeDtypeStruct((B, KV_A, V_ROWS, S), BF16),
        jax.ShapeDtypeStruct((B, S, Q_B), BF16),
        jax.ShapeDtypeStruct((B, S, KVD_B), BF16),
        jax.ShapeDtypeStruct((B, S, KVD_B), BF16),
    )
    return pl.pallas_call(
        _inproj_kernel,
        grid=(B, ns),
        in_specs=[
            pl.BlockSpec((1, tm, D), lambda b, s: (b, s, 0)),
            _resident((D, n_cols)),
            _resident((1, Q_A)),
            _resident((1, KVD_A)),
            pl.BlockSpec((tm, LANES), lambda b, s: (s, 0)),
            pl.BlockSpec((tm, LANES), lambda b, s: (s, 0)),
            _resident((LANES, LANES)),
        ],
        out_specs=(
            pl.BlockSpec((1, HEADS_A, HEAD_DIM, tm), lambda b, s: (b, 0, 0, s)),
            pl.BlockSpec((1, KV_A, tm, HEAD_DIM), lambda b, s: (b, 0, s, 0)),
            pl.BlockSpec((1, KV_A, V_ROWS, tm), lambda b, s: (b, 0, 0, s)),
            pl.BlockSpec((1, tm, Q_B), lambda b, s: (b, s, 0)),
            pl.BlockSpec((1, tm, KVD_B), lambda b, s: (b, s, 0)),
            pl.BlockSpec((1, tm, KVD_B), lambda b, s: (b, s, 0)),
        ),
        out_shape=out_shape,
        compiler_params=_cparams(("parallel", "parallel")),
        name="inproj",
    )(x, w, gq, gk, cos_t, sin_t, ones_blk)


def _gattn_kernel(qt_ref, k_ref, vt_ref, o_ref, m_sc, acc_sc):
    ki = pl.program_id(3)
    nk = pl.num_programs(3)

    @pl.when(ki == 0)
    def _():
        m_sc[...] = jnp.full(m_sc.shape, -jnp.inf, F32)
        acc_sc[...] = jnp.zeros(acc_sc.shape, F32)

    k = k_ref[0, 0]
    vt = vt_ref[0, 0]
    for h in range(G_A):
        st = jnp.dot(k, qt_ref[0, h], preferred_element_type=F32)
        m_prev = m_sc[h]
        m_new = jnp.maximum(m_prev, jnp.max(st, axis=0, keepdims=True))
        alpha = jnp.exp2(m_prev - m_new)
        pt = jnp.exp2(st - m_new).astype(BF16)
        acc_sc[h] = alpha * acc_sc[h] + jnp.dot(vt, pt, preferred_element_type=F32)
        m_sc[h] = m_new

    @pl.when(ki == nk - 1)
    def _():
        outs = []
        for h in range(G_A):
            a = acc_sc[h]
            outs.append(a[0:HEAD_DIM] / a[HEAD_DIM:HEAD_DIM + 1])
        o_ref[0] = jnp.concatenate(outs, axis=0).T.astype(o_ref.dtype)


def _gattn(qat, ka, vat, *, tq, tk):
    B, _, _, S = qat.shape
    return pl.pallas_call(
        _gattn_kernel,
        grid=(B, KV_A, S // tq, S // tk),
        in_specs=[
            pl.BlockSpec((1, G_A, HEAD_DIM, tq), lambda b, g, qi, ki: (b, g, 0, qi)),
            pl.BlockSpec((1, 1, tk, HEAD_DIM), lambda b, g, qi, ki: (b, g, ki, 0)),
            pl.BlockSpec((1, 1, V_ROWS, tk), lambda b, g, qi, ki: (b, g, 0, ki)),
        ],
        out_specs=pl.BlockSpec((1, tq, G_A * HEAD_DIM), lambda b, g, qi, ki: (b, qi, g)),
        out_shape=jax.ShapeDtypeStruct((B, S, Q_A), BF16),
        scratch_shapes=[
            pltpu.VMEM((G_A, 1, tq), F32),
            pltpu.VMEM((G_A, V_ROWS, tq), F32),
        ],
        compiler_params=_cparams(("parallel", "parallel", "parallel", "arbitrary")),
        name="gattn",
    )(qat, ka, vat)


def _wattn_kernel(rb_ref, sink_ref, bucket_ref, q_ref, kp_ref, kc_ref, kn_ref,
                  vp_ref, vc_ref, vn_ref, o_ref, bias_sc):
    b = pl.program_id(0)
    j = pl.program_id(1)
    nb = pl.num_programs(1)
    kw = 3 * Q_BLOCK

    @pl.when((b == 0) & (j == 0))
    def _():
        bucket = bucket_ref[...]
        qpos = lax.broadcasted_iota(jnp.int32, (Q_BLOCK, kw), 0)
        kpos = lax.broadcasted_iota(jnp.int32, (Q_BLOCK, kw), 1) - Q_BLOCK
        in_window = jnp.abs(kpos - qpos) <= WINDOW
        for h in range(HEADS_B):
            t = jnp.zeros((Q_BLOCK, kw), F32)
            for bk in range(N_BUCKETS):
                t = jnp.where(bucket == bk, rb_ref[bk, h], t)
            bias_sc[h] = jnp.where(in_window, t * LOG2E, -jnp.inf)

    col = lax.broadcasted_iota(jnp.int32, (1, kw), 1)
    out_of_seq = ((col < Q_BLOCK) & (j == 0)) | ((col >= 2 * Q_BLOCK) & (j == nb - 1))

    q = q_ref[0]
    k3 = jnp.concatenate([kp_ref[0], kc_ref[0], kn_ref[0]], axis=0)
    v3 = jnp.concatenate([vp_ref[0], vc_ref[0], vn_ref[0]], axis=0)
    outs = []
    for h in range(HEADS_B):
        g = h // G_B
        qh = q[:, h * HEAD_DIM:(h + 1) * HEAD_DIM]
        kh = k3[:, g * HEAD_DIM:(g + 1) * HEAD_DIM]
        vh = v3[:, g * HEAD_DIM:(g + 1) * HEAD_DIM]
        s = lax.dot_general(qh, kh, (((1,), (1,)), ((), ())), preferred_element_type=F32)
        s = jnp.where(out_of_seq, -jnp.inf, s + bias_sc[h])
        sink = sink_ref[h] * LOG2E
        m = jnp.maximum(jnp.max(s, axis=1, keepdims=True), sink)
        p = jnp.exp2(s - m)
        denom = jnp.sum(p, axis=1, keepdims=True) + jnp.exp2(sink - m)
        o = jnp.dot(p.astype(BF16), vh, preferred_element_type=F32)
        outs.append(o / denom)
    o_ref[0] = jnp.concatenate(outs, axis=1).astype(o_ref.dtype)


def _wattn(rel_bias, sink, bucket, qb, kb, vb):
    B, S, _ = qb.shape
    nb = S // Q_BLOCK
    kw = 3 * Q_BLOCK
    smem = pl.BlockSpec(memory_space=pltpu.SMEM)
    prev_map = lambda b, j: (b, jnp.maximum(j - 1, 0), 0)
    cur_map = lambda b, j: (b, j, 0)
    next_map = lambda b, j: (b, jnp.minimum(j + 1, nb - 1), 0)
    kv_blk = (1, Q_BLOCK, KVD_B)
    return pl.pallas_call(
        _wattn_kernel,
        grid=(B, nb),
        in_specs=[
            smem, smem,
            _resident((Q_BLOCK, kw)),
            pl.BlockSpec((1, Q_BLOCK, Q_B), cur_map),
            pl.BlockSpec(kv_blk, prev_map), pl.BlockSpec(kv_blk, cur_map), pl.BlockSpec(kv_blk, next_map),
            pl.BlockSpec(kv_blk, prev_map), pl.BlockSpec(kv_blk, cur_map), pl.BlockSpec(kv_blk, next_map),
        ],
        out_specs=pl.BlockSpec((1, Q_BLOCK, Q_B), cur_map),
        out_shape=jax.ShapeDtypeStruct((B, S, Q_B), BF16),
        scratch_shapes=[pltpu.VMEM((HEADS_B, Q_BLOCK, kw), F32)],
        compiler_params=_cparams(("arbitrary", "arbitrary")),
        name="wattn",
    )(rel_bias, sink, bucket, qb, kb, kb, kb, vb, vb, vb)


def _layer_norm(y, g, b):
    mu = jnp.mean(y, axis=-1, keepdims=True)
    d = y - mu
    var = jnp.mean(d * d, axis=-1, keepdims=True)
    return d * lax.rsqrt(var + LN_EPS) * g + b


def _rms(y, g):
    yf = y.astype(F32)
    return yf * lax.rsqrt(jnp.mean(yf * yf, axis=-1, keepdims=True) + RMS_EPS) * g


def _outproj_kernel(alpha, ya_ref, yb_ref, x_ref, ga_ref, gb_ref, wa_ref, wb_ref, lg_ref, lb_ref, o_ref):
    na = _rms(ya_ref[0], ga_ref[...]).astype(BF16)
    nb = _rms(yb_ref[0], gb_ref[...]).astype(BF16)
    mix = (jnp.dot(na, wa_ref[...], preferred_element_type=F32)
           + jnp.dot(nb, wb_ref[...], preferred_element_type=F32))
    o_ref[0] = _layer_norm(alpha * x_ref[0] + mix, lg_ref[...], lb_ref[...])


def _outproj(ya, yb, x, ga, gb, wa, wb, lg, lb, *, alpha, tm):
    B, S, D = x.shape
    row = lambda w: pl.BlockSpec((1, tm, w), lambda b, s: (b, s, 0))
    return pl.pallas_call(
        functools.partial(_outproj_kernel, alpha),
        grid=(B, S // tm),
        in_specs=[row(Q_A), row(Q_B), row(D), _resident((1, Q_A)), _resident((1, Q_B)),
                  _resident((Q_A, D)), _resident((Q_B, D)), _resident((1, D)), _resident((1, D))],
        out_specs=row(D),
        out_shape=jax.ShapeDtypeStruct((B, S, D), F32),
        compiler_params=_cparams(("parallel", "parallel")),
        name="outproj",
    )(ya, yb, x, ga, gb, wa, wb, lg, lb)


def _gelu_tanh(x):
    return 0.5 * x * (1.0 + jnp.tanh(math.sqrt(2.0 / math.pi) * (x + 0.044715 * (x * x * x))))


def _ffn_kernel(alpha, ff_chunk, x_ref, xp_ref, xn_ref, wg_ref, wu_ref, cw_ref, cb_ref, wd_ref,
                lg_ref, lb_ref, o_ref, h_sc):
    s = pl.program_id(1)
    ns = pl.num_programs(1)
    tm = x_ref.shape[1]
    d_ff = wg_ref.shape[1]
    halo = xp_ref.shape[1]
    x = x_ref[0]
    xb = x.astype(BF16)
    xh = jnp.concatenate([xp_ref[0], xn_ref[0]], axis=0).astype(BF16)
    has_prev = (s > 0).astype(F32)
    has_next = (s < ns - 1).astype(F32)
    row = lax.broadcasted_iota(jnp.int32, (tm, 1), 0)
    for c in range(d_ff // ff_chunk):
        cs = slice(c * ff_chunk, (c + 1) * ff_chunk)
        wg = wg_ref[:, cs]
        g = jnp.dot(xb, wg, preferred_element_type=F32)
        u = jnp.dot(xb, wu_ref[:, cs], preferred_element_type=F32)
        gh = jnp.dot(xh, wg, preferred_element_type=F32)
        g_before = gh[halo - 1:halo] * has_prev
        g_after = gh[halo:halo + 1] * has_next
        g_m1 = jnp.where(row == 0, g_before, pltpu.roll(g, 1, 0))
        g_p1 = jnp.where(row == tm - 1, g_after, pltpu.roll(g, tm - 1, 0))
        gc = cb_ref[:, cs] + g_m1 * cw_ref[0:1, cs] + g * cw_ref[1:2, cs] + g_p1 * cw_ref[2:3, cs]
        h_sc[:, cs] = (_gelu_tanh(gc) * u).astype(BF16)
    ffn = jnp.dot(h_sc[...], wd_ref[...], preferred_element_type=F32)
    o_ref[0] = _layer_norm(alpha * x + ffn, lg_ref[...], lb_ref[...])


def _ffn(x, wg, wu, cw, cb, wd, lg, lb, *, alpha, tm, ff_chunk, halo=8):
    B, S, D = x.shape
    d_ff = wg.shape[1]
    ns = S // tm
    hb = tm // halo
    last = S // halo - 1
    return pl.pallas_call(
        functools.partial(_ffn_kernel, alpha, ff_chunk),
        grid=(B, ns),
        in_specs=[
            pl.BlockSpec((1, tm, D), lambda b, s: (b, s, 0)),
            pl.BlockSpec((1, halo, D), lambda b, s: (b, jnp.maximum(s * hb - 1, 0), 0)),
            pl.BlockSpec((1, halo, D), lambda b, s: (b, jnp.minimum((s + 1) * hb, last), 0)),
            _resident((D, d_ff)), _resident((D, d_ff)), _resident((CONV_W, d_ff)), _resident((1, d_ff)),
            _resident((d_ff, D)), _resident((1, D)), _resident((1, D)),
        ],
        out_specs=pl.BlockSpec((1, tm, D), lambda b, s: (b, s, 0)),
        out_shape=jax.ShapeDtypeStruct((B, S, D), F32),
        scratch_shapes=[pltpu.VMEM((tm, d_ff), BF16)],
        compiler_params=_cparams(("parallel", "parallel")),
        name="ffn",
    )(x, x, x, wg, wu, cw, cb, wd, lg, lb)


def _rope_tables(seq_len):
    rows_n = seq_len // GRID_W
    row = jnp.repeat(jnp.arange(rows_n, dtype=F32), GRID_W)
    col = jnp.tile(jnp.arange(GRID_W, dtype=F32), rows_n)
    half = HEAD_DIM // 2
    inv_freq = ROPE_THETA ** (-jnp.arange(0, half, 2, dtype=F32) / half)
    ang = jnp.concatenate([row[:, None] * inv_freq, col[:, None] * inv_freq], axis=-1)
    c, s = jnp.cos(ang), jnp.sin(ang)
    return jnp.concatenate([c, c, c, c], axis=-1), jnp.concatenate([-s, s, -s, s], axis=-1)


def _t5_bucket(rel):
    half = N_BUCKETS // 2
    max_exact = half // 2
    bucket = jnp.where(rel > 0, half, 0)
    rp = jnp.abs(rel)
    rpf = jnp.maximum(rp, 1).astype(F32)
    large = max_exact + (jnp.log(rpf / max_exact) / math.log(MAX_DISTANCE / max_exact)
                         * (half - max_exact)).astype(jnp.int32)
    large = jnp.minimum(large, half - 1)
    return bucket + jnp.where(rp < max_exact, rp, large)


def _deinterleave_perm(n_heads):
    per_head = np.concatenate([np.arange(0, HEAD_DIM, 2), np.arange(1, HEAD_DIM, 2)])
    return np.concatenate([h * HEAD_DIM + per_head for h in range(n_heads)])


def kernel(x, rel_bias, w_in, q_norm, k_norm, sink, out_norm_a, out_norm_b, w_out,
           ln1_g, ln1_b, w_gate, w_up, conv_w, conv_b, w_down, ln2_g, ln2_b):
    B, S, D = x.shape
    depth = w_in.shape[0]
    alpha = (2.0 * depth) ** 0.25

    cos_t, sin_t = _rope_tables(S)
    qpos = jnp.arange(Q_BLOCK, dtype=jnp.int32)
    kpos = jnp.arange(3 * Q_BLOCK, dtype=jnp.int32) - Q_BLOCK
    bucket = _t5_bucket(kpos[None, :] - qpos[:, None]).astype(jnp.int32)
    blk = np.arange(LANES) // HEAD_DIM
    ones_blk = jnp.asarray(blk[:, None] == blk[None, :], dtype=BF16)

    perm_q = _deinterleave_perm(HEADS_A)
    perm_k = _deinterleave_perm(KV_A)
    col_perm = np.concatenate([perm_q, Q_A + perm_k, np.arange(Q_A + KVD_A, w_in.shape[2])])
    head_perm = _deinterleave_perm(1)

    for l in range(depth):
        w = w_in[l][:, col_perm].astype(BF16)
        gq = jnp.tile(q_norm[l][head_perm] * (SM_SCALE * LOG2E), HEADS_A)[None, :]
        gk = jnp.tile(k_norm[l][head_perm], KV_A)[None, :]
        qat, ka, vat, qb, kb, vb = _inproj(x, w, gq, gk, cos_t, sin_t, ones_blk, tm=512)
        ya = _gattn(qat, ka, vat, tq=256, tk=2048)
        yb = _wattn(rel_bias, sink[l], bucket, qb, kb, vb)
        x = _outproj(ya, yb, x, out_norm_a[l][None, :], out_norm_b[l][None, :],
                     w_out[l][:Q_A].astype(BF16), w_out[l][Q_A:].astype(BF16),
                     ln1_g[l][None, :], ln1_b[l][None, :], alpha=alpha, tm=512)
        x = _ffn(x, w_gate[l].astype(BF16), w_up[l].astype(BF16), conv_w[l], conv_b[l][None, :],
                 w_down[l].astype(BF16), ln2_g[l][None, :], ln2_b[l][None, :],
                 alpha=alpha, tm=512, ff_chunk=256)
    return x
```

```python
import functools
import math

import jax
import jax.numpy as jnp
import numpy as np
from jax import lax
from jax.experimental import pallas as pl
from jax.experimental.pallas import tpu as pltpu

F32 = jnp.float32
BF16 = jnp.bfloat16

HEAD_DIM = 64
HEADS_A = 8
HEADS_B = 8
KV_A = 2
KV_B = 2
G_A = HEADS_A // KV_A
G_B = HEADS_B // KV_B
Q_A = HEADS_A * HEAD_DIM
KVD_A = KV_A * HEAD_DIM
Q_B = HEADS_B * HEAD_DIM
KVD_B = KV_B * HEAD_DIM
GRID_W = 64
ROPE_THETA = 10000.0
Q_BLOCK = 128
WINDOW = 128
N_BUCKETS = 32
MAX_DISTANCE = 128
CONV_W = 3
RMS_EPS = 1e-6
LN_EPS = 1e-5
LOG2E = math.log2(math.e)
SM_SCALE = HEAD_DIM ** -0.5

LANES = 128
V_ROWS = 80
VMEM_LIMIT = 56 * 1024 * 1024


def _cparams(sem):
    return pltpu.CompilerParams(dimension_semantics=sem, vmem_limit_bytes=VMEM_LIMIT)


def _resident(shape):
    nd = len(shape)
    return pl.BlockSpec(shape, lambda *_: (0,) * nd, pipeline_mode=pl.Buffered(1))


def _inproj_kernel(x_ref, w_ref, gq_ref, gk_ref, cos_ref, sin_ref, j_ref,
                   qat_ref, ka_ref, vat_ref, qb_ref, kb_ref, vb_ref):
    tm = x_ref.shape[1]
    x = x_ref[0].astype(BF16)
    h = jnp.dot(x, w_ref[...], preferred_element_type=F32)

    lane = lax.broadcasted_iota(jnp.int32, (tm, LANES), 1)
    first_half = (lane % HEAD_DIM) < (HEAD_DIM // 2)
    cosv = cos_ref[...]
    sinv = sin_ref[...]
    ones_blk = j_ref[...]

    def norm_rope(hc, gain):
        ssq = jnp.dot((hc * hc).astype(BF16), ones_blk, preferred_element_type=F32)
        y = hc * lax.rsqrt(ssq * (1.0 / HEAD_DIM) + RMS_EPS) * gain
        partner = jnp.where(first_half, pltpu.roll(y, LANES - HEAD_DIM // 2, 1),
                            pltpu.roll(y, HEAD_DIM // 2, 1))
        return y * cosv + partner * sinv

    q = jnp.concatenate(
        [norm_rope(h[:, c * LANES:(c + 1) * LANES], gq_ref[:, c * LANES:(c + 1) * LANES])
         for c in range(Q_A // LANES)], axis=1)
    qt = q.T
    for hh in range(HEADS_A):
        qat_ref[0, hh] = qt[hh * HEAD_DIM:(hh + 1) * HEAD_DIM].astype(BF16)

    k = norm_rope(h[:, Q_A:Q_A + KVD_A], gk_ref[...])
    v = h[:, Q_A + KVD_A:Q_A + 2 * KVD_A]
    vt = v.T
    row = lax.broadcasted_iota(jnp.int32, (V_ROWS - HEAD_DIM, tm), 0)
    ones_rows = jnp.where(row == 0, 1.0, 0.0).astype(BF16)
    for g in range(KV_A):
        ka_ref[0, g] = k[:, g * HEAD_DIM:(g + 1) * HEAD_DIM].astype(BF16)
        vat_ref[0, g, 0, 0:HEAD_DIM, :] = vt[g * HEAD_DIM:(g + 1) * HEAD_DIM].astype(BF16)
        vat_ref[0, g, 0, HEAD_DIM:V_ROWS, :] = ones_rows

    b0 = Q_A + 2 * KVD_A
    qb_ref[0] = (h[:, b0:b0 + Q_B] * (SM_SCALE * LOG2E)).astype(BF16)
    kb_ref[0] = h[:, b0 + Q_B:b0 + Q_B + KVD_B].astype(BF16)
    vb_ref[0] = h[:, b0 + Q_B + KVD_B:b0 + Q_B + 2 * KVD_B].astype(BF16)


def _inproj(x, w, gq, gk, cos_t, sin_t, ones_blk, *, tm, tkc):
    B, S, D = x.shape
    ns = S // tm
    n_cols = w.shape[1]
    per_chunk = tkc // tm
    out_shape = (
        jax.ShapeDtypeStruct((B, HEADS_A, HEAD_DIM, S), BF16),
        jax.ShapeDtypeStruct((B, KV_A, S, HEAD_DIM), BF16),
        jax.ShapeDtypeStruct((B, KV_A, S // tkc, V_ROWS, tkc), BF16),
        jax.ShapeDtypeStruct((B, S, Q_B), BF16),
        jax.ShapeDtypeStruct((B, S, KVD_B), BF16),
        jax.ShapeDtypeStruct((B, S, KVD_B), BF16),
    )
    return pl.pallas_call(
        _inproj_kernel,
        grid=(B, ns),
        in_specs=[
            pl.BlockSpec((1, tm, D), lambda b, s: (b, s, 0)),
            _resident((D, n_cols)),
            _resident((1, Q_A)),
            _resident((1, KVD_A)),
            pl.BlockSpec((tm, LANES), lambda b, s: (s, 0)),
            pl.BlockSpec((tm, LANES), lambda b, s: (s, 0)),
            _resident((LANES, LANES)),
        ],
        out_specs=(
            pl.BlockSpec((1, HEADS_A, HEAD_DIM, tm), lambda b, s: (b, 0, 0, s)),
            pl.BlockSpec((1, KV_A, tm, HEAD_DIM), lambda b, s: (b, 0, s, 0)),
            pl.BlockSpec((1, KV_A, 1, V_ROWS, tm), lambda b, s: (b, 0, s // per_chunk, 0, s % per_chunk)),
            pl.BlockSpec((1, tm, Q_B), lambda b, s: (b, s, 0)),
            pl.BlockSpec((1, tm, KVD_B), lambda b, s: (b, s, 0)),
            pl.BlockSpec((1, tm, KVD_B), lambda b, s: (b, s, 0)),
        ),
        out_shape=out_shape,
        compiler_params=_cparams(("parallel", "parallel")),
        name="inproj",
    )(x, w, gq, gk, cos_t, sin_t, ones_blk)


def _gattn_kernel(qt_ref, k_ref, vt_ref, o_ref, s_sc, mnew_sc, alpha_sc, m_sc, acc_sc):
    n_chunks, _, tkc = vt_ref.shape[2:]
    assert n_chunks % 2 == 0 and n_chunks >= 4

    m_sc[...] = jnp.full(m_sc.shape, -jnp.inf, F32)
    acc_sc[...] = jnp.zeros(acc_sc.shape, F32)

    def scores(j, buf):
        kc = k_ref[0, 0, pl.ds(pl.multiple_of(j * tkc, tkc), tkc), :]
        for h in range(G_A):
            st = jnp.dot(kc, qt_ref[0, h], preferred_element_type=F32)
            s_sc[buf, h] = st
            m_prev = m_sc[h]
            m_new = jnp.maximum(m_prev, jnp.max(st, axis=0, keepdims=True))
            alpha_sc[buf, h] = jnp.exp2(m_prev - m_new)
            mnew_sc[buf, h] = m_new
            m_sc[h] = m_new

    def accumulate(j, buf):
        vt = vt_ref[0, 0, j]
        for h in range(G_A):
            pt = jnp.exp2(s_sc[buf, h] - mnew_sc[buf, h]).astype(BF16)
            acc_sc[h] = alpha_sc[buf, h] * acc_sc[h] + jnp.dot(vt, pt, preferred_element_type=F32)

    scores(0, 0)

    def pair(i, carry):
        scores(2 * i + 1, 1)
        accumulate(2 * i, 0)
        scores(2 * i + 2, 0)
        accumulate(2 * i + 1, 1)
        return carry

    lax.fori_loop(0, n_chunks // 2 - 1, pair, 0)
    scores(n_chunks - 1, 1)
    accumulate(n_chunks - 2, 0)
    accumulate(n_chunks - 1, 1)

    outs = []
    for h in range(G_A):
        a = acc_sc[h]
        outs.append(a[0:HEAD_DIM] / a[HEAD_DIM:HEAD_DIM + 1])
    o_ref[0] = jnp.concatenate(outs, axis=0).T.astype(o_ref.dtype)


def _gattn(qat, ka, vat, *, tq):
    B, _, _, S = qat.shape
    n_chunks, _, tkc = vat.shape[2:]
    return pl.pallas_call(
        _gattn_kernel,
        grid=(B, KV_A, S // tq),
        in_specs=[
            pl.BlockSpec((1, G_A, HEAD_DIM, tq), lambda b, g, qi: (b, g, 0, qi)),
            pl.BlockSpec((1, 1, S, HEAD_DIM), lambda b, g, qi: (b, g, 0, 0)),
            pl.BlockSpec((1, 1, n_chunks, V_ROWS, tkc), lambda b, g, qi: (b, g, 0, 0, 0)),
        ],
        out_specs=pl.BlockSpec((1, tq, G_A * HEAD_DIM), lambda b, g, qi: (b, qi, g)),
        out_shape=jax.ShapeDtypeStruct((B, S, Q_A), BF16),
        scratch_shapes=[
            pltpu.VMEM((2, G_A, tkc, tq), F32),
            pltpu.VMEM((2, G_A, 1, tq), F32),
            pltpu.VMEM((2, G_A, 1, tq), F32),
            pltpu.VMEM((G_A, 1, tq), F32),
            pltpu.VMEM((G_A, V_ROWS, tq), F32),
        ],
        compiler_params=_cparams(("parallel", "parallel", "arbitrary")),
        name="gattn",
    )(qat, ka, vat)


def _wattn_kernel(rb_ref, sink_ref, bucket_ref, q_ref, kp_ref, kc_ref, kn_ref,
                  vp_ref, vc_ref, vn_ref, o_ref, bias_sc):
    b = pl.program_id(0)
    j = pl.program_id(1)
    nb = pl.num_programs(1)
    kw = 3 * Q_BLOCK

    @pl.when((b == 0) & (j == 0))
    def _():
        bucket = bucket_ref[...]
        qpos = lax.broadcasted_iota(jnp.int32, (Q_BLOCK, kw), 0)
        kpos = lax.broadcasted_iota(jnp.int32, (Q_BLOCK, kw), 1) - Q_BLOCK
        in_window = jnp.abs(kpos - qpos) <= WINDOW
        for h in range(HEADS_B):
            t = jnp.zeros((Q_BLOCK, kw), F32)
            for bk in range(N_BUCKETS):
                t = jnp.where(bucket == bk, rb_ref[bk, h], t)
            bias_sc[h] = jnp.where(in_window, t * LOG2E, -jnp.inf)

    col = lax.broadcasted_iota(jnp.int32, (1, kw), 1)
    out_of_seq = ((col < Q_BLOCK) & (j == 0)) | ((col >= 2 * Q_BLOCK) & (j == nb - 1))

    q = q_ref[0]
    k3 = jnp.concatenate([kp_ref[0], kc_ref[0], kn_ref[0]], axis=0)
    v3 = jnp.concatenate([vp_ref[0], vc_ref[0], vn_ref[0]], axis=0)
    outs = []
    for h in range(HEADS_B):
        g = h // G_B
        qh = q[:, h * HEAD_DIM:(h + 1) * HEAD_DIM]
        kh = k3[:, g * HEAD_DIM:(g + 1) * HEAD_DIM]
        vh = v3[:, g * HEAD_DIM:(g + 1) * HEAD_DIM]
        s = lax.dot_general(qh, kh, (((1,), (1,)), ((), ())), preferred_element_type=F32)
        s = jnp.where(out_of_seq, -jnp.inf, s + bias_sc[h])
        sink = sink_ref[h] * LOG2E
        m = jnp.maximum(jnp.max(s, axis=1, keepdims=True), sink)
        p = jnp.exp2(s - m)
        denom = jnp.sum(p, axis=1, keepdims=True) + jnp.exp2(sink - m)
        o = jnp.dot(p.astype(BF16), vh, preferred_element_type=F32)
        outs.append(o / denom)
    o_ref[0] = jnp.concatenate(outs, axis=1).astype(o_ref.dtype)


def _wattn(rel_bias, sink, bucket, qb, kb, vb):
    B, S, _ = qb.shape
    nb = S // Q_BLOCK
    kw = 3 * Q_BLOCK
    smem = pl.BlockSpec(memory_space=pltpu.SMEM)
    prev_map = lambda b, j: (b, jnp.maximum(j - 1, 0), 0)
    cur_map = lambda b, j: (b, j, 0)
    next_map = lambda b, j: (b, jnp.minimum(j + 1, nb - 1), 0)
    kv_blk = (1, Q_BLOCK, KVD_B)
    return pl.pallas_call(
        _wattn_kernel,
        grid=(B, nb),
        in_specs=[
            smem, smem,
            _resident((Q_BLOCK, kw)),
            pl.BlockSpec((1, Q_BLOCK, Q_B), cur_map),
            pl.BlockSpec(kv_blk, prev_map), pl.BlockSpec(kv_blk, cur_map), pl.BlockSpec(kv_blk, next_map),
            pl.BlockSpec(kv_blk, prev_map), pl.BlockSpec(kv_blk, cur_map), pl.BlockSpec(kv_blk, next_map),
        ],
        out_specs=pl.BlockSpec((1, Q_BLOCK, Q_B), cur_map),
        out_shape=jax.ShapeDtypeStruct((B, S, Q_B), BF16),
        scratch_shapes=[pltpu.VMEM((HEADS_B, Q_BLOCK, kw), F32)],
        compiler_params=_cparams(("arbitrary", "arbitrary")),
        name="wattn",
    )(rel_bias, sink, bucket, qb, kb, kb, kb, vb, vb, vb)


def _layer_norm(y, g, b):
    mu = jnp.mean(y, axis=-1, keepdims=True)
    d = y - mu
    var = jnp.mean(d * d, axis=-1, keepdims=True)
    return d * lax.rsqrt(var + LN_EPS) * g + b


def _rms(y, g):
    yf = y.astype(F32)
    return yf * lax.rsqrt(jnp.mean(yf * yf, axis=-1, keepdims=True) + RMS_EPS) * g


def _outproj_kernel(alpha, ya_ref, yb_ref, x_ref, ga_ref, gb_ref, wa_ref, wb_ref, lg_ref, lb_ref, o_ref):
    na = _rms(ya_ref[0], ga_ref[...]).astype(BF16)
    nb = _rms(yb_ref[0], gb_ref[...]).astype(BF16)
    mix = (jnp.dot(na, wa_ref[...], preferred_element_type=F32)
           + jnp.dot(nb, wb_ref[...], preferred_element_type=F32))
    o_ref[0] = _layer_norm(alpha * x_ref[0] + mix, lg_ref[...], lb_ref[...])


def _outproj(ya, yb, x, ga, gb, wa, wb, lg, lb, *, alpha, tm):
    B, S, D = x.shape
    row = lambda w: pl.BlockSpec((1, tm, w), lambda b, s: (b, s, 0))
    return pl.pallas_call(
        functools.partial(_outproj_kernel, alpha),
        grid=(B, S // tm),
        in_specs=[row(Q_A), row(Q_B), row(D), _resident((1, Q_A)), _resident((1, Q_B)),
                  _resident((Q_A, D)), _resident((Q_B, D)), _resident((1, D)), _resident((1, D))],
        out_specs=row(D),
        out_shape=jax.ShapeDtypeStruct((B, S, D), F32),
        compiler_params=_cparams(("parallel", "parallel")),
        name="outproj",
    )(ya, yb, x, ga, gb, wa, wb, lg, lb)


def _gelu_tanh(x):
    return 0.5 * x * (1.0 + jnp.tanh(math.sqrt(2.0 / math.pi) * (x + 0.044715 * (x * x * x))))


def _ffn_kernel(alpha, ff_chunk, x_ref, xp_ref, xn_ref, wg_ref, wu_ref, cw_ref, cb_ref, wd_ref,
                lg_ref, lb_ref, o_ref, h_sc):
    s = pl.program_id(1)
    ns = pl.num_programs(1)
    tm = x_ref.shape[1]
    d_ff = wg_ref.shape[1]
    halo = xp_ref.shape[1]
    x = x_ref[0]
    xb = x.astype(BF16)
    xh = jnp.concatenate([xp_ref[0], xn_ref[0]], axis=0).astype(BF16)
    has_prev = (s > 0).astype(F32)
    has_next = (s < ns - 1).astype(F32)
    row = lax.broadcasted_iota(jnp.int32, (tm, 1), 0)
    for c in range(d_ff // ff_chunk):
        cs = slice(c * ff_chunk, (c + 1) * ff_chunk)
        wg = wg_ref[:, cs]
        g = jnp.dot(xb, wg, preferred_element_type=F32)
        u = jnp.dot(xb, wu_ref[:, cs], preferred_element_type=F32)
        gh = jnp.dot(xh, wg, preferred_element_type=F32)
        g_before = gh[halo - 1:halo] * has_prev
        g_after = gh[halo:halo + 1] * has_next
        g_m1 = jnp.where(row == 0, g_before, pltpu.roll(g, 1, 0))
        g_p1 = jnp.where(row == tm - 1, g_after, pltpu.roll(g, tm - 1, 0))
        gc = cb_ref[:, cs] + g_m1 * cw_ref[0:1, cs] + g * cw_ref[1:2, cs] + g_p1 * cw_ref[2:3, cs]
        h_sc[:, cs] = (_gelu_tanh(gc) * u).astype(BF16)
    ffn = jnp.dot(h_sc[...], wd_ref[...], preferred_element_type=F32)
    o_ref[0] = _layer_norm(alpha * x + ffn, lg_ref[...], lb_ref[...])


def _ffn(x, wg, wu, cw, cb, wd, lg, lb, *, alpha, tm, ff_chunk, halo=8):
    B, S, D = x.shape
    d_ff = wg.shape[1]
    ns = S // tm
    hb = tm // halo
    last = S // halo - 1
    return pl.pallas_call(
        functools.partial(_ffn_kernel, alpha, ff_chunk),
        grid=(B, ns),
        in_specs=[
            pl.BlockSpec((1, tm, D), lambda b, s: (b, s, 0)),
            pl.BlockSpec((1, halo, D), lambda b, s: (b, jnp.maximum(s * hb - 1, 0), 0)),
            pl.BlockSpec((1, halo, D), lambda b, s: (b, jnp.minimum((s + 1) * hb, last), 0)),
            _resident((D, d_ff)), _resident((D, d_ff)), _resident((CONV_W, d_ff)), _resident((1, d_ff)),
            _resident((d_ff, D)), _resident((1, D)), _resident((1, D)),
        ],
        out_specs=pl.BlockSpec((1, tm, D), lambda b, s: (b, s, 0)),
        out_shape=jax.ShapeDtypeStruct((B, S, D), F32),
        scratch_shapes=[pltpu.VMEM((tm, d_ff), BF16)],
        compiler_params=_cparams(("parallel", "parallel")),
        name="ffn",
    )(x, x, x, wg, wu, cw, cb, wd, lg, lb)


def _rope_tables(seq_len):
    rows_n = seq_len // GRID_W
    row = jnp.repeat(jnp.arange(rows_n, dtype=F32), GRID_W)
    col = jnp.tile(jnp.arange(GRID_W, dtype=F32), rows_n)
    half = HEAD_DIM // 2
    inv_freq = ROPE_THETA ** (-jnp.arange(0, half, 2, dtype=F32) / half)
    ang = jnp.concatenate([row[:, None] * inv_freq, col[:, None] * inv_freq], axis=-1)
    c, s = jnp.cos(ang), jnp.sin(ang)
    return jnp.concatenate([c, c, c, c], axis=-1), jnp.concatenate([-s, s, -s, s], axis=-1)


def _t5_bucket(rel):
    half = N_BUCKETS // 2
    max_exact = half // 2
    bucket = jnp.where(rel > 0, half, 0)
    rp = jnp.abs(rel)
    rpf = jnp.maximum(rp, 1).astype(F32)
    large = max_exact + (jnp.log(rpf / max_exact) / math.log(MAX_DISTANCE / max_exact)
                         * (half - max_exact)).astype(jnp.int32)
    large = jnp.minimum(large, half - 1)
    return bucket + jnp.where(rp < max_exact, rp, large)


def _deinterleave_perm(n_heads):
    per_head = np.concatenate([np.arange(0, HEAD_DIM, 2), np.arange(1, HEAD_DIM, 2)])
    return np.concatenate([h * HEAD_DIM + per_head for h in range(n_heads)])


def kernel(x, rel_bias, w_in, q_norm, k_norm, sink, out_norm_a, out_norm_b, w_out,
           ln1_g, ln1_b, w_gate, w_up, conv_w, conv_b, w_down, ln2_g, ln2_b):
    B, S, D = x.shape
    depth = w_in.shape[0]
    alpha = (2.0 * depth) ** 0.25

    cos_t, sin_t = _rope_tables(S)
    qpos = jnp.arange(Q_BLOCK, dtype=jnp.int32)
    kpos = jnp.arange(3 * Q_BLOCK, dtype=jnp.int32) - Q_BLOCK
    bucket = _t5_bucket(kpos[None, :] - qpos[:, None]).astype(jnp.int32)
    blk = np.arange(LANES) // HEAD_DIM
    ones_blk = jnp.asarray(blk[:, None] == blk[None, :], dtype=BF16)

    perm_q = _deinterleave_perm(HEADS_A)
    perm_k = _deinterleave_perm(KV_A)
    col_perm = np.concatenate([perm_q, Q_A + perm_k, np.arange(Q_A + KVD_A, w_in.shape[2])])
    head_perm = _deinterleave_perm(1)

    for l in range(depth):
        w = w_in[l][:, col_perm].astype(BF16)
        gq = jnp.tile(q_norm[l][head_perm] * (SM_SCALE * LOG2E), HEADS_A)[None, :]
        gk = jnp.tile(k_norm[l][head_perm], KV_A)[None, :]
        qat, ka, vat, qb, kb, vb = _inproj(x, w, gq, gk, cos_t, sin_t, ones_blk, tm=512, tkc=1024)
        ya = _gattn(qat, ka, vat, tq=256)
        yb = _wattn(rel_bias, sink[l], bucket, qb, kb, vb)
        x = _outproj(ya, yb, x, out_norm_a[l][None, :], out_norm_b[l][None, :],
                     w_out[l][:Q_A].astype(BF16), w_out[l][Q_A:].astype(BF16),
                     ln1_g[l][None, :], ln1_b[l][None, :], alpha=alpha, tm=512)
        x = _ffn(x, w_gate[l].astype(BF16), w_up[l].astype(BF16), conv_w[l], conv_b[l][None, :],
                 w_down[l].astype(BF16), ln2_g[l][None, :], ln2_b[l][None, :],
                 alpha=alpha, tm=512, ff_chunk=256)
    return x
```

```python
import functools
import math

import jax
import jax.numpy as jnp
import numpy as np
from jax import lax
from jax.experimental import pallas as pl
from jax.experimental.pallas import tpu as pltpu

F32 = jnp.float32
BF16 = jnp.bfloat16

HEAD_DIM = 64
HEADS_A = 8
HEADS_B = 8
KV_A = 2
KV_B = 2
G_A = HEADS_A // KV_A
G_B = HEADS_B // KV_B
Q_A = HEADS_A * HEAD_DIM
KVD_A = KV_A * HEAD_DIM
Q_B = HEADS_B * HEAD_DIM
KVD_B = KV_B * HEAD_DIM
GRID_W = 64
ROPE_THETA = 10000.0
Q_BLOCK = 128
WINDOW = 128
N_BUCKETS = 32
MAX_DISTANCE = 128
CONV_W = 3
RMS_EPS = 1e-6
LN_EPS = 1e-5
LOG2E = math.log2(math.e)
SM_SCALE = HEAD_DIM ** -0.5

LANES = 128
V_ROWS = 80
G_PIECE = 256
VMEM_LIMIT = 56 * 1024 * 1024


def _cparams(sem):
    return pltpu.CompilerParams(dimension_semantics=sem, vmem_limit_bytes=VMEM_LIMIT)


def _resident(shape):
    nd = len(shape)
    return pl.BlockSpec(shape, lambda *_: (0,) * nd, pipeline_mode=pl.Buffered(1))


def _inproj_kernel(x_ref, w_ref, gq_ref, gk_ref, cos_ref, sin_ref, j_ref,
                   qat_ref, ka_ref, vat_ref, qbt_ref, kb_ref, vbt_ref):
    tm = x_ref.shape[1]
    x = x_ref[0].astype(BF16)
    h = jnp.dot(x, w_ref[...], preferred_element_type=F32)

    lane = lax.broadcasted_iota(jnp.int32, (tm, LANES), 1)
    first_half = (lane % HEAD_DIM) < (HEAD_DIM // 2)
    cosv = cos_ref[...]
    sinv = sin_ref[...]
    ones_blk = j_ref[...]

    def norm_rope(hc, gain):
        ssq = jnp.dot((hc * hc).astype(BF16), ones_blk, preferred_element_type=F32)
        y = hc * lax.rsqrt(ssq * (1.0 / HEAD_DIM) + RMS_EPS) * gain
        partner = jnp.where(first_half, pltpu.roll(y, LANES - HEAD_DIM // 2, 1),
                            pltpu.roll(y, HEAD_DIM // 2, 1))
        return y * cosv + partner * sinv

    q = jnp.concatenate(
        [norm_rope(h[:, c * LANES:(c + 1) * LANES], gq_ref[:, c * LANES:(c + 1) * LANES])
         for c in range(Q_A // LANES)], axis=1)
    qt = q.T
    for hh in range(HEADS_A):
        qat_ref[0, hh] = qt[hh * HEAD_DIM:(hh + 1) * HEAD_DIM].astype(BF16)

    k = norm_rope(h[:, Q_A:Q_A + KVD_A], gk_ref[...])
    v = h[:, Q_A + KVD_A:Q_A + 2 * KVD_A]
    vt = v.T
    row = lax.broadcasted_iota(jnp.int32, (V_ROWS - HEAD_DIM, tm), 0)
    ones_rows = jnp.where(row == 0, 1.0, 0.0).astype(BF16)
    for g in range(KV_A):
        ka_ref[0, g] = k[:, g * HEAD_DIM:(g + 1) * HEAD_DIM].astype(BF16)
        vat_ref[0, g, 0, 0:HEAD_DIM, :] = vt[g * HEAD_DIM:(g + 1) * HEAD_DIM].astype(BF16)
        vat_ref[0, g, 0, HEAD_DIM:V_ROWS, :] = ones_rows

    b0 = Q_A + 2 * KVD_A
    qbt = (h[:, b0:b0 + Q_B] * (SM_SCALE * LOG2E)).T
    for hh in range(HEADS_B):
        qbt_ref[0, hh] = qbt[hh * HEAD_DIM:(hh + 1) * HEAD_DIM].astype(BF16)
    kb = h[:, b0 + Q_B:b0 + Q_B + KVD_B]
    vbt = h[:, b0 + Q_B + KVD_B:b0 + Q_B + 2 * KVD_B].T
    for g in range(KV_B):
        kb_ref[0, g] = kb[:, g * HEAD_DIM:(g + 1) * HEAD_DIM].astype(BF16)
        vbt_ref[0, g, 0:HEAD_DIM, :] = vbt[g * HEAD_DIM:(g + 1) * HEAD_DIM].astype(BF16)
        vbt_ref[0, g, HEAD_DIM:V_ROWS, :] = ones_rows


def _inproj(x, w, gq, gk, cos_t, sin_t, ones_blk, *, tm, tkc):
    B, S, D = x.shape
    ns = S // tm
    n_cols = w.shape[1]
    per_chunk = tkc // tm
    out_shape = (
        jax.ShapeDtypeStruct((B, HEADS_A, HEAD_DIM, S), BF16),
        jax.ShapeDtypeStruct((B, KV_A, S, HEAD_DIM), BF16),
        jax.ShapeDtypeStruct((B, KV_A, S // tkc, V_ROWS, tkc), BF16),
        jax.ShapeDtypeStruct((B, HEADS_B, HEAD_DIM, S), BF16),
        jax.ShapeDtypeStruct((B, KV_B, S, HEAD_DIM), BF16),
        jax.ShapeDtypeStruct((B, KV_B, V_ROWS, S), BF16),
    )
    return pl.pallas_call(
        _inproj_kernel,
        grid=(B, ns),
        in_specs=[
            pl.BlockSpec((1, tm, D), lambda b, s: (b, s, 0)),
            _resident((D, n_cols)),
            _resident((1, Q_A)),
            _resident((1, KVD_A)),
            pl.BlockSpec((tm, LANES), lambda b, s: (s, 0)),
            pl.BlockSpec((tm, LANES), lambda b, s: (s, 0)),
            _resident((LANES, LANES)),
        ],
        out_specs=(
            pl.BlockSpec((1, HEADS_A, HEAD_DIM, tm), lambda b, s: (b, 0, 0, s)),
            pl.BlockSpec((1, KV_A, tm, HEAD_DIM), lambda b, s: (b, 0, s, 0)),
            pl.BlockSpec((1, KV_A, 1, V_ROWS, tm), lambda b, s: (b, 0, s // per_chunk, 0, s % per_chunk)),
            pl.BlockSpec((1, HEADS_B, HEAD_DIM, tm), lambda b, s: (b, 0, 0, s)),
            pl.BlockSpec((1, KV_B, tm, HEAD_DIM), lambda b, s: (b, 0, s, 0)),
            pl.BlockSpec((1, KV_B, V_ROWS, tm), lambda b, s: (b, 0, 0, s)),
        ),
        out_shape=out_shape,
        compiler_params=_cparams(("parallel", "parallel")),
        name="inproj",
    )(x, w, gq, gk, cos_t, sin_t, ones_blk)


def _gattn_kernel(qt_ref, k_ref, vt_ref, o_ref, s_sc, mnew_sc, alpha_sc, m_sc, acc_sc):
    n_chunks, _, tkc = vt_ref.shape[2:]
    assert n_chunks % 2 == 0 and n_chunks >= 4

    m_sc[...] = jnp.full(m_sc.shape, -jnp.inf, F32)
    acc_sc[...] = jnp.zeros(acc_sc.shape, F32)

    n_pieces = tkc // G_PIECE

    def score_piece(j, buf, h, pc):
        rows = pl.ds(pl.multiple_of(j * tkc, tkc) + pc * G_PIECE, G_PIECE)
        st = jnp.dot(k_ref[0, 0, rows, :], qt_ref[0, h], preferred_element_type=F32)
        s_sc[buf, h, pc * G_PIECE:(pc + 1) * G_PIECE, :] = st
        return jnp.max(st, axis=0, keepdims=True)

    def value_piece(j, buf, h, pc):
        cols = slice(pc * G_PIECE, (pc + 1) * G_PIECE)
        pt = jnp.exp2(s_sc[buf, h, cols, :] - mnew_sc[buf, h]).astype(BF16)
        return jnp.dot(vt_ref[0, 0, j, :, cols], pt, preferred_element_type=F32)

    def stage(score_of, value_of):
        for h in range(G_A):
            piece_max, pv = [], None
            for pc in range(n_pieces):
                if score_of is not None:
                    piece_max.append(score_piece(*score_of, h, pc))
                if value_of is not None:
                    d = value_piece(*value_of, h, pc)
                    pv = d if pv is None else pv + d
            if score_of is not None:
                buf = score_of[1]
                m_prev = m_sc[h]
                m_new = functools.reduce(jnp.maximum, piece_max, m_prev)
                alpha_sc[buf, h] = jnp.exp2(m_prev - m_new)
                mnew_sc[buf, h] = m_new
                m_sc[h] = m_new
            if value_of is not None:
                acc_sc[h] = alpha_sc[value_of[1], h] * acc_sc[h] + pv

    stage((0, 0), None)

    def pair(i, carry):
        stage((2 * i + 1, 1), (2 * i, 0))
        stage((2 * i + 2, 0), (2 * i + 1, 1))
        return carry

    lax.fori_loop(0, n_chunks // 2 - 1, pair, 0)
    stage((n_chunks - 1, 1), (n_chunks - 2, 0))
    stage(None, (n_chunks - 1, 1))

    outs = []
    for h in range(G_A):
        a = acc_sc[h]
        outs.append(a[0:HEAD_DIM] / a[HEAD_DIM:HEAD_DIM + 1])
    o_ref[0] = jnp.concatenate(outs, axis=0).T.astype(o_ref.dtype)


def _gattn(qat, ka, vat, *, tq):
    B, _, _, S = qat.shape
    n_chunks, _, tkc = vat.shape[2:]
    return pl.pallas_call(
        _gattn_kernel,
        grid=(B, KV_A, S // tq),
        in_specs=[
            pl.BlockSpec((1, G_A, HEAD_DIM, tq), lambda b, g, qi: (b, g, 0, qi)),
            pl.BlockSpec((1, 1, S, HEAD_DIM), lambda b, g, qi: (b, g, 0, 0)),
            pl.BlockSpec((1, 1, n_chunks, V_ROWS, tkc), lambda b, g, qi: (b, g, 0, 0, 0)),
        ],
        out_specs=pl.BlockSpec((1, tq, G_A * HEAD_DIM), lambda b, g, qi: (b, qi, g)),
        out_shape=jax.ShapeDtypeStruct((B, S, Q_A), BF16),
        scratch_shapes=[
            pltpu.VMEM((2, G_A, tkc, tq), F32),
            pltpu.VMEM((2, G_A, 1, tq), F32),
            pltpu.VMEM((2, G_A, 1, tq), F32),
            pltpu.VMEM((G_A, 1, tq), F32),
            pltpu.VMEM((G_A, V_ROWS, tq), F32),
        ],
        compiler_params=_cparams(("parallel", "parallel", "arbitrary")),
        name="gattn",
    )(qat, ka, vat)


W_TQ = 2 * Q_BLOCK
W_TK = W_TQ + 2 * WINDOW
W_NBLK = W_TK // Q_BLOCK


def _wattn_kernel(rb_ref, sink_ref, bucket_ref, qt_ref, *refs):
    k_refs = refs[:W_NBLK]
    v_refs = refs[W_NBLK:2 * W_NBLK]
    o_ref, bias_sc, s_sc = refs[2 * W_NBLK:]
    b = pl.program_id(0)
    i = pl.program_id(1)
    n = pl.num_programs(1)

    @pl.when((b == 0) & (i == 0))
    def _():
        bucket = bucket_ref[...]
        krow = lax.broadcasted_iota(jnp.int32, (W_TK, W_TQ), 0)
        qcol = lax.broadcasted_iota(jnp.int32, (W_TK, W_TQ), 1)
        in_window = jnp.abs(krow - WINDOW - qcol) <= WINDOW
        for h in range(HEADS_B):
            t = jnp.zeros((W_TK, W_TQ), F32)
            for bk in range(N_BUCKETS):
                t = jnp.where(bucket == bk, rb_ref[bk, h], t)
            t = jnp.where(in_window, t * LOG2E, -jnp.inf)
            bias_sc[0, h] = jnp.where(krow >= Q_BLOCK, t, -jnp.inf)
            bias_sc[1, h] = t
            bias_sc[2, h] = jnp.where(krow < W_TK - Q_BLOCK, t, -jnp.inf)

    table = jnp.where(i == 0, 0, jnp.where(i == n - 1, 2, 1))
    kws = [jnp.concatenate([r[0, g] for r in k_refs], axis=0) for g in range(KV_B)]
    vws = [jnp.concatenate([r[0, g] for r in v_refs], axis=1) for g in range(KV_B)]
    sinks = [sink_ref[h] * LOG2E for h in range(HEADS_B)]
    maxes = [None] * HEADS_B

    def scores(h):
        st = jnp.dot(kws[h // G_B], qt_ref[0, h], preferred_element_type=F32) + bias_sc[table, h]
        s_sc[h] = st
        maxes[h] = jnp.maximum(jnp.max(st, axis=0, keepdims=True), sinks[h])

    def finish(h):
        pt = jnp.exp2(s_sc[h] - maxes[h]).astype(BF16)
        acc = jnp.dot(vws[h // G_B], pt, preferred_element_type=F32)
        denom = acc[HEAD_DIM:HEAD_DIM + 1] + jnp.exp2(sinks[h] - maxes[h])
        return acc[0:HEAD_DIM] / denom

    ahead = 2
    for h in range(ahead):
        scores(h)
    outs = []
    for h in range(HEADS_B):
        if h + ahead < HEADS_B:
            scores(h + ahead)
        outs.append(finish(h))
    o_ref[0] = jnp.concatenate(outs, axis=0).T.astype(o_ref.dtype)


def _wattn(rel_bias, sink, bucket, qbt, kb, vbt):
    B, _, _, S = qbt.shape
    n = S // W_TQ
    last_blk = S // Q_BLOCK - 1
    smem = pl.BlockSpec(memory_space=pltpu.SMEM)

    def blk(c):
        return lambda b, i: jnp.clip(i * (W_TQ // Q_BLOCK) - 1 + c, 0, last_blk)

    k_specs = [pl.BlockSpec((1, KV_B, Q_BLOCK, HEAD_DIM), lambda b, i, f=blk(c): (b, 0, f(b, i), 0))
               for c in range(W_NBLK)]
    v_specs = [pl.BlockSpec((1, KV_B, V_ROWS, Q_BLOCK), lambda b, i, f=blk(c): (b, 0, 0, f(b, i)))
               for c in range(W_NBLK)]
    return pl.pallas_call(
        _wattn_kernel,
        grid=(B, n),
        in_specs=[smem, smem, _resident((W_TK, W_TQ)),
                  pl.BlockSpec((1, HEADS_B, HEAD_DIM, W_TQ), lambda b, i: (b, 0, 0, i))] + k_specs + v_specs,
        out_specs=pl.BlockSpec((1, W_TQ, Q_B), lambda b, i: (b, i, 0)),
        out_shape=jax.ShapeDtypeStruct((B, S, Q_B), BF16),
        scratch_shapes=[pltpu.VMEM((3, HEADS_B, W_TK, W_TQ), F32), pltpu.VMEM((HEADS_B, W_TK, W_TQ), F32)],
        compiler_params=_cparams(("arbitrary", "arbitrary")),
        name="wattn",
    )(rel_bias, sink, bucket, qbt, *([kb] * W_NBLK), *([vbt] * W_NBLK))


def _layer_norm(y, g, b):
    mu = jnp.mean(y, axis=-1, keepdims=True)
    d = y - mu
    var = jnp.mean(d * d, axis=-1, keepdims=True)
    return d * lax.rsqrt(var + LN_EPS) * g + b


def _rms(y, g):
    yf = y.astype(F32)
    return yf * lax.rsqrt(jnp.mean(yf * yf, axis=-1, keepdims=True) + RMS_EPS) * g


def _outproj_kernel(alpha, ya_ref, yb_ref, x_ref, ga_ref, gb_ref, wa_ref, wb_ref, lg_ref, lb_ref, o_ref):
    na = _rms(ya_ref[0], ga_ref[...]).astype(BF16)
    nb = _rms(yb_ref[0], gb_ref[...]).astype(BF16)
    mix = (jnp.dot(na, wa_ref[...], preferred_element_type=F32)
           + jnp.dot(nb, wb_ref[...], preferred_element_type=F32))
    o_ref[0] = _layer_norm(alpha * x_ref[0] + mix, lg_ref[...], lb_ref[...])


def _outproj(ya, yb, x, ga, gb, wa, wb, lg, lb, *, alpha, tm):
    B, S, D = x.shape
    row = lambda w: pl.BlockSpec((1, tm, w), lambda b, s: (b, s, 0))
    return pl.pallas_call(
        functools.partial(_outproj_kernel, alpha),
        grid=(B, S // tm),
        in_specs=[row(Q_A), row(Q_B), row(D), _resident((1, Q_A)), _resident((1, Q_B)),
                  _resident((Q_A, D)), _resident((Q_B, D)), _resident((1, D)), _resident((1, D))],
        out_specs=row(D),
        out_shape=jax.ShapeDtypeStruct((B, S, D), F32),
        compiler_params=_cparams(("parallel", "parallel")),
        name="outproj",
    )(ya, yb, x, ga, gb, wa, wb, lg, lb)


def _gelu_tanh(x):
    return 0.5 * x * (1.0 + jnp.tanh(math.sqrt(2.0 / math.pi) * (x + 0.044715 * (x * x * x))))


def _ffn_kernel(alpha, ff_chunk, x_ref, xp_ref, xn_ref, wg_ref, wu_ref, cw_ref, cb_ref, wd_ref,
                lg_ref, lb_ref, o_ref, h_sc):
    s = pl.program_id(1)
    ns = pl.num_programs(1)
    tm = x_ref.shape[1]
    d_ff = wg_ref.shape[1]
    halo = xp_ref.shape[1]
    x = x_ref[0]
    xb = x.astype(BF16)
    xh = jnp.concatenate([xp_ref[0], xn_ref[0]], axis=0).astype(BF16)
    has_prev = (s > 0).astype(F32)
    has_next = (s < ns - 1).astype(F32)
    row = lax.broadcasted_iota(jnp.int32, (tm, 1), 0)
    for c in range(d_ff // ff_chunk):
        cs = slice(c * ff_chunk, (c + 1) * ff_chunk)
        wg = wg_ref[:, cs]
        g = jnp.dot(xb, wg, preferred_element_type=F32)
        u = jnp.dot(xb, wu_ref[:, cs], preferred_element_type=F32)
        gh = jnp.dot(xh, wg, preferred_element_type=F32)
        g_before = gh[halo - 1:halo] * has_prev
        g_after = gh[halo:halo + 1] * has_next
        g_m1 = jnp.where(row == 0, g_before, pltpu.roll(g, 1, 0))
        g_p1 = jnp.where(row == tm - 1, g_after, pltpu.roll(g, tm - 1, 0))
        gc = cb_ref[:, cs] + g_m1 * cw_ref[0:1, cs] + g * cw_ref[1:2, cs] + g_p1 * cw_ref[2:3, cs]
        h_sc[:, cs] = (_gelu_tanh(gc) * u).astype(BF16)
    ffn = jnp.dot(h_sc[...], wd_ref[...], preferred_element_type=F32)
    o_ref[0] = _layer_norm(alpha * x + ffn, lg_ref[...], lb_ref[...])


def _ffn(x, wg, wu, cw, cb, wd, lg, lb, *, alpha, tm, ff_chunk, halo=8):
    B, S, D = x.shape
    d_ff = wg.shape[1]
    ns = S // tm
    hb = tm // halo
    last = S // halo - 1
    return pl.pallas_call(
        functools.partial(_ffn_kernel, alpha, ff_chunk),
        grid=(B, ns),
        in_specs=[
            pl.BlockSpec((1, tm, D), lambda b, s: (b, s, 0)),
            pl.BlockSpec((1, halo, D), lambda b, s: (b, jnp.maximum(s * hb - 1, 0), 0)),
            pl.BlockSpec((1, halo, D), lambda b, s: (b, jnp.minimum((s + 1) * hb, last), 0)),
            _resident((D, d_ff)), _resident((D, d_ff)), _resident((CONV_W, d_ff)), _resident((1, d_ff)),
            _resident((d_ff, D)), _resident((1, D)), _resident((1, D)),
        ],
        out_specs=pl.BlockSpec((1, tm, D), lambda b, s: (b, s, 0)),
        out_shape=jax.ShapeDtypeStruct((B, S, D), F32),
        scratch_shapes=[pltpu.VMEM((tm, d_ff), BF16)],
        compiler_params=_cparams(("parallel", "parallel")),
        name="ffn",
    )(x, x, x, wg, wu, cw, cb, wd, lg, lb)


def _rope_tables(seq_len):
    rows_n = seq_len // GRID_W
    row = jnp.repeat(jnp.arange(rows_n, dtype=F32), GRID_W)
    col = jnp.tile(jnp.arange(GRID_W, dtype=F32), rows_n)
    half = HEAD_DIM // 2
    inv_freq = ROPE_THETA ** (-jnp.arange(0, half, 2, dtype=F32) / half)
    ang = jnp.concatenate([row[:, None] * inv_freq, col[:, None] * inv_freq], axis=-1)
    c, s = jnp.cos(ang), jnp.sin(ang)
    return jnp.concatenate([c, c, c, c], axis=-1), jnp.concatenate([-s, s, -s, s], axis=-1)


def _t5_bucket(rel):
    half = N_BUCKETS // 2
    max_exact = half // 2
    bucket = jnp.where(rel > 0, half, 0)
    rp = jnp.abs(rel)
    rpf = jnp.maximum(rp, 1).astype(F32)
    large = max_exact + (jnp.log(rpf / max_exact) / math.log(MAX_DISTANCE / max_exact)
                         * (half - max_exact)).astype(jnp.int32)
    large = jnp.minimum(large, half - 1)
    return bucket + jnp.where(rp < max_exact, rp, large)


def _deinterleave_perm(n_heads):
    per_head = np.concatenate([np.arange(0, HEAD_DIM, 2), np.arange(1, HEAD_DIM, 2)])
    return np.concatenate([h * HEAD_DIM + per_head for h in range(n_heads)])


def kernel(x, rel_bias, w_in, q_norm, k_norm, sink, out_norm_a, out_norm_b, w_out,
           ln1_g, ln1_b, w_gate, w_up, conv_w, conv_b, w_down, ln2_g, ln2_b):
    B, S, D = x.shape
    depth = w_in.shape[0]
    alpha = (2.0 * depth) ** 0.25

    cos_t, sin_t = _rope_tables(S)
    qpos = jnp.arange(W_TQ, dtype=jnp.int32)
    kpos = jnp.arange(W_TK, dtype=jnp.int32) - WINDOW
    bucket = _t5_bucket(kpos[:, None] - qpos[None, :]).astype(jnp.int32)
    blk = np.arange(LANES) // HEAD_DIM
    ones_blk = jnp.asarray(blk[:, None] == blk[None, :], dtype=BF16)

    perm_q = _deinterleave_perm(HEADS_A)
    perm_k = _deinterleave_perm(KV_A)
    col_perm = np.concatenate([perm_q, Q_A + perm_k, np.arange(Q_A + KVD_A, w_in.shape[2])])
    head_perm = _deinterleave_perm(1)

    for l in range(depth):
        w = w_in[l][:, col_perm].astype(BF16)
        gq = jnp.tile(q_norm[l][head_perm] * (SM_SCALE * LOG2E), HEADS_A)[None, :]
        gk = jnp.tile(k_norm[l][head_perm], KV_A)[None, :]
        qat, ka, vat, qbt, kb, vbt = _inproj(x, w, gq, gk, cos_t, sin_t, ones_blk, tm=512, tkc=1024)
        ya = _gattn(qat, ka, vat, tq=256)
        yb = _wattn(rel_bias, sink[l], bucket, qbt, kb, vbt)
        x = _outproj(ya, yb, x, out_norm_a[l][None, :], out_norm_b[l][None, :],
                     w_out[l][:Q_A].astype(BF16), w_out[l][Q_A:].astype(BF16),
                     ln1_g[l][None, :], ln1_b[l][None, :], alpha=alpha, tm=512)
        x = _ffn(x, w_gate[l].astype(BF16), w_up[l].astype(BF16), conv_w[l], conv_b[l][None, :],
                 w_down[l].astype(BF16), ln2_g[l][None, :], ln2_b[l][None, :],
                 alpha=alpha, tm=512, ff_chunk=256)
    return x
```

```python
import functools
import math

import jax
import jax.numpy as jnp
import numpy as np
from jax import lax
from jax.experimental import pallas as pl
from jax.experimental.pallas import tpu as pltpu

F32 = jnp.float32
BF16 = jnp.bfloat16

HEAD_DIM = 64
HEADS_A = 8
HEADS_B = 8
KV_A = 2
KV_B = 2
G_A = HEADS_A // KV_A
G_B = HEADS_B // KV_B
Q_A = HEADS_A * HEAD_DIM
KVD_A = KV_A * HEAD_DIM
Q_B = HEADS_B * HEAD_DIM
KVD_B = KV_B * HEAD_DIM
GRID_W = 64
ROPE_THETA = 10000.0
Q_BLOCK = 128
WINDOW = 128
N_BUCKETS = 32
MAX_DISTANCE = 128
CONV_W = 3
RMS_EPS = 1e-6
LN_EPS = 1e-5
LOG2E = math.log2(math.e)
SM_SCALE = HEAD_DIM ** -0.5

LANES = 128
V_ROWS = 80
G_PIECE = 256
VMEM_LIMIT = 56 * 1024 * 1024


def _cparams(sem):
    return pltpu.CompilerParams(dimension_semantics=sem, vmem_limit_bytes=VMEM_LIMIT)


def _resident(shape):
    nd = len(shape)
    return pl.BlockSpec(shape, lambda *_: (0,) * nd, pipeline_mode=pl.Buffered(1))


def _inproj_kernel(x_ref, w_ref, gq_ref, gk_ref, cos_ref, sin_ref, j_ref,
                   qat_ref, ka_ref, vat_ref, qbt_ref, kb_ref, vbt_ref):
    tm = x_ref.shape[1]
    x = x_ref[0].astype(BF16)
    h = jnp.dot(x, w_ref[...], preferred_element_type=F32)

    lane = lax.broadcasted_iota(jnp.int32, (tm, LANES), 1)
    first_half = (lane % HEAD_DIM) < (HEAD_DIM // 2)
    cosv = cos_ref[...]
    sinv = sin_ref[...]
    ones_blk = j_ref[...]

    def norm_rope(hc, gain):
        ssq = jnp.dot((hc * hc).astype(BF16), ones_blk, preferred_element_type=F32)
        y = hc * lax.rsqrt(ssq * (1.0 / HEAD_DIM) + RMS_EPS) * gain
        partner = jnp.where(first_half, pltpu.roll(y, LANES - HEAD_DIM // 2, 1),
                            pltpu.roll(y, HEAD_DIM // 2, 1))
        return y * cosv + partner * sinv

    q = jnp.concatenate(
        [norm_rope(h[:, c * LANES:(c + 1) * LANES], gq_ref[:, c * LANES:(c + 1) * LANES])
         for c in range(Q_A // LANES)], axis=1)
    qt = q.T
    tq = qat_ref.shape[4]
    for hh in range(HEADS_A):
        for c in range(tm // tq):
            qat_ref[0, hh, c] = qt[hh * HEAD_DIM:(hh + 1) * HEAD_DIM, c * tq:(c + 1) * tq].astype(BF16)

    k = norm_rope(h[:, Q_A:Q_A + KVD_A], gk_ref[...])
    v = h[:, Q_A + KVD_A:Q_A + 2 * KVD_A]
    vt = v.T
    row = lax.broadcasted_iota(jnp.int32, (V_ROWS - HEAD_DIM, tm), 0)
    ones_rows = jnp.where(row == 0, 1.0, 0.0).astype(BF16)
    for g in range(KV_A):
        ka_ref[0, g] = k[:, g * HEAD_DIM:(g + 1) * HEAD_DIM].astype(BF16)
        vat_ref[0, g, 0, 0:HEAD_DIM, :] = vt[g * HEAD_DIM:(g + 1) * HEAD_DIM].astype(BF16)
        vat_ref[0, g, 0, HEAD_DIM:V_ROWS, :] = ones_rows

    b0 = Q_A + 2 * KVD_A
    qbt = (h[:, b0:b0 + Q_B] * (SM_SCALE * LOG2E)).T
    for hh in range(HEADS_B):
        qbt_ref[0, hh] = qbt[hh * HEAD_DIM:(hh + 1) * HEAD_DIM].astype(BF16)
    kb = h[:, b0 + Q_B:b0 + Q_B + KVD_B]
    vbt = h[:, b0 + Q_B + KVD_B:b0 + Q_B + 2 * KVD_B].T
    for g in range(KV_B):
        kb_ref[0, g] = kb[:, g * HEAD_DIM:(g + 1) * HEAD_DIM].astype(BF16)
        vbt_ref[0, g, 0:HEAD_DIM, :] = vbt[g * HEAD_DIM:(g + 1) * HEAD_DIM].astype(BF16)
        vbt_ref[0, g, HEAD_DIM:V_ROWS, :] = ones_rows


def _inproj(x, w, gq, gk, cos_t, sin_t, ones_blk, *, tm, tkc, tq):
    B, S, D = x.shape
    ns = S // tm
    n_cols = w.shape[1]
    per_chunk = tkc // tm
    out_shape = (
        jax.ShapeDtypeStruct((B, HEADS_A, S // tq, HEAD_DIM, tq), BF16),
        jax.ShapeDtypeStruct((B, KV_A, S, HEAD_DIM), BF16),
        jax.ShapeDtypeStruct((B, KV_A, S // tkc, V_ROWS, tkc), BF16),
        jax.ShapeDtypeStruct((B, HEADS_B, HEAD_DIM, S), BF16),
        jax.ShapeDtypeStruct((B, KV_B, S, HEAD_DIM), BF16),
        jax.ShapeDtypeStruct((B, KV_B, V_ROWS, S), BF16),
    )
    return pl.pallas_call(
        _inproj_kernel,
        grid=(B, ns),
        in_specs=[
            pl.BlockSpec((1, tm, D), lambda b, s: (b, s, 0)),
            _resident((D, n_cols)),
            _resident((1, Q_A)),
            _resident((1, KVD_A)),
            pl.BlockSpec((tm, LANES), lambda b, s: (s, 0)),
            pl.BlockSpec((tm, LANES), lambda b, s: (s, 0)),
            _resident((LANES, LANES)),
        ],
        out_specs=(
            pl.BlockSpec((1, HEADS_A, tm // tq, HEAD_DIM, tq), lambda b, s: (b, 0, s, 0, 0)),
            pl.BlockSpec((1, KV_A, tm, HEAD_DIM), lambda b, s: (b, 0, s, 0)),
            pl.BlockSpec((1, KV_A, 1, V_ROWS, tm), lambda b, s: (b, 0, s // per_chunk, 0, s % per_chunk)),
            pl.BlockSpec((1, HEADS_B, HEAD_DIM, tm), lambda b, s: (b, 0, 0, s)),
            pl.BlockSpec((1, KV_B, tm, HEAD_DIM), lambda b, s: (b, 0, s, 0)),
            pl.BlockSpec((1, KV_B, V_ROWS, tm), lambda b, s: (b, 0, 0, s)),
        ),
        out_shape=out_shape,
        compiler_params=_cparams(("parallel", "parallel")),
        name="inproj",
    )(x, w, gq, gk, cos_t, sin_t, ones_blk)


def _gattn_kernel(qt_ref, k_ref, vt_ref, o_ref, s_sc, mnew_sc, alpha_sc, m_sc, acc_sc):
    n_chunks, _, tkc = vt_ref.shape[2:]
    nq, _, tq = qt_ref.shape[2:]
    assert n_chunks % 2 == 0 and n_chunks >= 4
    n_pieces = tkc // G_PIECE

    def reset_max():
        m_sc[...] = jnp.full(m_sc.shape, -jnp.inf, F32)

    def score_piece(qi, j, buf, h, pc):
        rows = pl.ds(pl.multiple_of(j * tkc, tkc) + pc * G_PIECE, G_PIECE)
        st = jnp.dot(k_ref[0, 0, rows, :], qt_ref[0, h, qi], preferred_element_type=F32)
        s_sc[buf, h, pc * G_PIECE:(pc + 1) * G_PIECE, :] = st
        return jnp.max(st, axis=0, keepdims=True)

    def value_piece(j, buf, h, pc):
        cols = slice(pc * G_PIECE, (pc + 1) * G_PIECE)
        pt = jnp.exp2(s_sc[buf, h, cols, :] - mnew_sc[buf, h]).astype(BF16)
        return jnp.dot(vt_ref[0, 0, j, :, cols], pt, preferred_element_type=F32)

    def stage(score_of, value_of):
        for h in range(G_A):
            piece_max, pv = [], None
            for pc in range(n_pieces):
                if score_of is not None:
                    piece_max.append(score_piece(*score_of, h, pc))
                if value_of is not None:
                    d = value_piece(*value_of, h, pc)
                    pv = d if pv is None else pv + d
            if score_of is not None:
                buf = score_of[2]
                m_prev = m_sc[h]
                m_new = functools.reduce(jnp.maximum, piece_max, m_prev)
                alpha_sc[buf, h] = jnp.exp2(m_prev - m_new)
                mnew_sc[buf, h] = m_new
                m_sc[h] = m_new
            if value_of is not None:
                acc_sc[h] = alpha_sc[value_of[1], h] * acc_sc[h] + pv

    def middle_chunks(qi):
        def pair(i, carry):
            stage((qi, 2 * i + 1, 1), (2 * i, 0))
            stage((qi, 2 * i + 2, 0), (2 * i + 1, 1))
            return carry

        lax.fori_loop(0, n_chunks // 2 - 1, pair, 0)
        stage((qi, n_chunks - 1, 1), (n_chunks - 2, 0))

    def finalize(qi):
        outs = []
        for h in range(G_A):
            a = acc_sc[h]
            outs.append(a[0:HEAD_DIM] / a[HEAD_DIM:HEAD_DIM + 1])
        rows = pl.ds(pl.multiple_of(qi * tq, tq), tq)
        o_ref[0, rows, :] = jnp.concatenate(outs, axis=0).T.astype(o_ref.dtype)

    acc_sc[...] = jnp.zeros(acc_sc.shape, F32)
    reset_max()
    stage((0, 0, 0), None)

    def q_block(qi, carry):
        middle_chunks(qi)
        reset_max()
        stage((qi + 1, 0, 0), (n_chunks - 1, 1))
        finalize(qi)
        return carry

    lax.fori_loop(0, nq - 1, q_block, 0)
    middle_chunks(nq - 1)
    stage(None, (n_chunks - 1, 1))
    finalize(nq - 1)


def _gattn(qat, ka, vat):
    B, _, nq, _, tq = qat.shape
    S = nq * tq
    n_chunks, _, tkc = vat.shape[2:]
    return pl.pallas_call(
        _gattn_kernel,
        grid=(B, KV_A),
        in_specs=[
            pl.BlockSpec((1, G_A, nq, HEAD_DIM, tq), lambda b, g: (b, g, 0, 0, 0)),
            pl.BlockSpec((1, 1, S, HEAD_DIM), lambda b, g: (b, g, 0, 0)),
            pl.BlockSpec((1, 1, n_chunks, V_ROWS, tkc), lambda b, g: (b, g, 0, 0, 0)),
        ],
        out_specs=pl.BlockSpec((1, S, G_A * HEAD_DIM), lambda b, g: (b, 0, g)),
        out_shape=jax.ShapeDtypeStruct((B, S, Q_A), BF16),
        scratch_shapes=[
            pltpu.VMEM((2, G_A, tkc, tq), F32),
            pltpu.VMEM((2, G_A, 1, tq), F32),
            pltpu.VMEM((2, G_A, 1, tq), F32),
            pltpu.VMEM((G_A, 1, tq), F32),
            pltpu.VMEM((G_A, V_ROWS, tq), F32),
        ],
        compiler_params=_cparams(("parallel", "parallel")),
        name="gattn",
    )(qat, ka, vat)


W_TQ = 2 * Q_BLOCK
W_TK = W_TQ + 2 * WINDOW
W_NBLK = W_TK // Q_BLOCK


def _wattn_kernel(rb_ref, sink_ref, bucket_ref, qt_ref, *refs):
    k_refs = refs[:W_NBLK]
    v_refs = refs[W_NBLK:2 * W_NBLK]
    o_ref, bias_sc, s_sc = refs[2 * W_NBLK:]
    b = pl.program_id(0)
    i = pl.program_id(1)
    n = pl.num_programs(1)

    @pl.when((b == 0) & (i == 0))
    def _():
        bucket = bucket_ref[...]
        krow = lax.broadcasted_iota(jnp.int32, (W_TK, W_TQ), 0)
        qcol = lax.broadcasted_iota(jnp.int32, (W_TK, W_TQ), 1)
        in_window = jnp.abs(krow - WINDOW - qcol) <= WINDOW
        for h in range(HEADS_B):
            t = jnp.zeros((W_TK, W_TQ), F32)
            for bk in range(N_BUCKETS):
                t = jnp.where(bucket == bk, rb_ref[bk, h], t)
            t = jnp.where(in_window, t * LOG2E, -jnp.inf)
            bias_sc[0, h] = jnp.where(krow >= Q_BLOCK, t, -jnp.inf)
            bias_sc[1, h] = t
            bias_sc[2, h] = jnp.where(krow < W_TK - Q_BLOCK, t, -jnp.inf)

    table = jnp.where(i == 0, 0, jnp.where(i == n - 1, 2, 1))
    kws = [jnp.concatenate([r[0, g] for r in k_refs], axis=0) for g in range(KV_B)]
    vws = [jnp.concatenate([r[0, g] for r in v_refs], axis=1) for g in range(KV_B)]
    sinks = [sink_ref[h] * LOG2E for h in range(HEADS_B)]
    maxes = [None] * HEADS_B

    def scores(h):
        st = jnp.dot(kws[h // G_B], qt_ref[0, h], preferred_element_type=F32) + bias_sc[table, h]
        s_sc[h] = st
        maxes[h] = jnp.maximum(jnp.max(st, axis=0, keepdims=True), sinks[h])

    def finish(h):
        pt = jnp.exp2(s_sc[h] - maxes[h]).astype(BF16)
        acc = jnp.dot(vws[h // G_B], pt, preferred_element_type=F32)
        denom = acc[HEAD_DIM:HEAD_DIM + 1] + jnp.exp2(sinks[h] - maxes[h])
        return acc[0:HEAD_DIM] / denom

    ahead = 2
    for h in range(ahead):
        scores(h)
    outs = []
    for h in range(HEADS_B):
        if h + ahead < HEADS_B:
            scores(h + ahead)
        outs.append(finish(h))
    o_ref[0] = jnp.concatenate(outs, axis=0).T.astype(o_ref.dtype)


def _wattn(rel_bias, sink, bucket, qbt, kb, vbt):
    B, _, _, S = qbt.shape
    n = S // W_TQ
    last_blk = S // Q_BLOCK - 1
    smem = pl.BlockSpec(memory_space=pltpu.SMEM)

    def blk(c):
        return lambda b, i: jnp.clip(i * (W_TQ // Q_BLOCK) - 1 + c, 0, last_blk)

    k_specs = [pl.BlockSpec((1, KV_B, Q_BLOCK, HEAD_DIM), lambda b, i, f=blk(c): (b, 0, f(b, i), 0))
               for c in range(W_NBLK)]
    v_specs = [pl.BlockSpec((1, KV_B, V_ROWS, Q_BLOCK), lambda b, i, f=blk(c): (b, 0, 0, f(b, i)))
               for c in range(W_NBLK)]
    return pl.pallas_call(
        _wattn_kernel,
        grid=(B, n),
        in_specs=[smem, smem, _resident((W_TK, W_TQ)),
                  pl.BlockSpec((1, HEADS_B, HEAD_DIM, W_TQ), lambda b, i: (b, 0, 0, i))] + k_specs + v_specs,
        out_specs=pl.BlockSpec((1, W_TQ, Q_B), lambda b, i: (b, i, 0)),
        out_shape=jax.ShapeDtypeStruct((B, S, Q_B), BF16),
        scratch_shapes=[pltpu.VMEM((3, HEADS_B, W_TK, W_TQ), F32), pltpu.VMEM((HEADS_B, W_TK, W_TQ), F32)],
        compiler_params=_cparams(("arbitrary", "arbitrary")),
        name="wattn",
    )(rel_bias, sink, bucket, qbt, *([kb] * W_NBLK), *([vbt] * W_NBLK))


def _layer_norm(y, g, b):
    mu = jnp.mean(y, axis=-1, keepdims=True)
    d = y - mu
    var = jnp.mean(d * d, axis=-1, keepdims=True)
    return d * lax.rsqrt(var + LN_EPS) * g + b


def _rms(y, g):
    yf = y.astype(F32)
    return yf * lax.rsqrt(jnp.mean(yf * yf, axis=-1, keepdims=True) + RMS_EPS) * g


def _outproj_kernel(alpha, ya_ref, yb_ref, x_ref, ga_ref, gb_ref, wa_ref, wb_ref, lg_ref, lb_ref, o_ref):
    na = _rms(ya_ref[0], ga_ref[...]).astype(BF16)
    nb = _rms(yb_ref[0], gb_ref[...]).astype(BF16)
    mix = (jnp.dot(na, wa_ref[...], preferred_element_type=F32)
           + jnp.dot(nb, wb_ref[...], preferred_element_type=F32))
    o_ref[0] = _layer_norm(alpha * x_ref[0] + mix, lg_ref[...], lb_ref[...])


def _outproj(ya, yb, x, ga, gb, wa, wb, lg, lb, *, alpha, tm):
    B, S, D = x.shape
    row = lambda w: pl.BlockSpec((1, tm, w), lambda b, s: (b, s, 0))
    return pl.pallas_call(
        functools.partial(_outproj_kernel, alpha),
        grid=(B, S // tm),
        in_specs=[row(Q_A), row(Q_B), row(D), _resident((1, Q_A)), _resident((1, Q_B)),
                  _resident((Q_A, D)), _resident((Q_B, D)), _resident((1, D)), _resident((1, D))],
        out_specs=row(D),
        out_shape=jax.ShapeDtypeStruct((B, S, D), F32),
        compiler_params=_cparams(("parallel", "parallel")),
        name="outproj",
    )(ya, yb, x, ga, gb, wa, wb, lg, lb)


def _gelu_tanh(x):
    return 0.5 * x * (1.0 + jnp.tanh(math.sqrt(2.0 / math.pi) * (x + 0.044715 * (x * x * x))))


def _ffn_kernel(alpha, ff_chunk, x_ref, xp_ref, xn_ref, wg_ref, wu_ref, cw_ref, cb_ref, wd_ref,
                lg_ref, lb_ref, o_ref, h_sc):
    s = pl.program_id(1)
    ns = pl.num_programs(1)
    tm = x_ref.shape[1]
    d_ff = wg_ref.shape[1]
    halo = xp_ref.shape[1]
    x = x_ref[0]
    xb = x.astype(BF16)
    xh = jnp.concatenate([xp_ref[0], xn_ref[0]], axis=0).astype(BF16)
    has_prev = (s > 0).astype(F32)
    has_next = (s < ns - 1).astype(F32)
    row = lax.broadcasted_iota(jnp.int32, (tm, 1), 0)
    for c in range(d_ff // ff_chunk):
        cs = slice(c * ff_chunk, (c + 1) * ff_chunk)
        wg = wg_ref[:, cs]
        g = jnp.dot(xb, wg, preferred_element_type=F32)
        u = jnp.dot(xb, wu_ref[:, cs], preferred_element_type=F32)
        gh = jnp.dot(xh, wg, preferred_element_type=F32)
        g_before = gh[halo - 1:halo] * has_prev
        g_after = gh[halo:halo + 1] * has_next
        g_m1 = jnp.where(row == 0, g_before, pltpu.roll(g, 1, 0))
        g_p1 = jnp.where(row == tm - 1, g_after, pltpu.roll(g, tm - 1, 0))
        gc = cb_ref[:, cs] + g_m1 * cw_ref[0:1, cs] + g * cw_ref[1:2, cs] + g_p1 * cw_ref[2:3, cs]
        h_sc[:, cs] = (_gelu_tanh(gc) * u).astype(BF16)
    ffn = jnp.dot(h_sc[...], wd_ref[...], preferred_element_type=F32)
    o_ref[0] = _layer_norm(alpha * x + ffn, lg_ref[...], lb_ref[...])


def _ffn(x, wg, wu, cw, cb, wd, lg, lb, *, alpha, tm, ff_chunk, halo=8):
    B, S, D = x.shape
    d_ff = wg.shape[1]
    ns = S // tm
    hb = tm // halo
    last = S // halo - 1
    return pl.pallas_call(
        functools.partial(_ffn_kernel, alpha, ff_chunk),
        grid=(B, ns),
        in_specs=[
            pl.BlockSpec((1, tm, D), lambda b, s: (b, s, 0)),
            pl.BlockSpec((1, halo, D), lambda b, s: (b, jnp.maximum(s * hb - 1, 0), 0)),
            pl.BlockSpec((1, halo, D), lambda b, s: (b, jnp.minimum((s + 1) * hb, last), 0)),
            _resident((D, d_ff)), _resident((D, d_ff)), _resident((CONV_W, d_ff)), _resident((1, d_ff)),
            _resident((d_ff, D)), _resident((1, D)), _resident((1, D)),
        ],
        out_specs=pl.BlockSpec((1, tm, D), lambda b, s: (b, s, 0)),
        out_shape=jax.ShapeDtypeStruct((B, S, D), F32),
        scratch_shapes=[pltpu.VMEM((tm, d_ff), BF16)],
        compiler_params=_cparams(("parallel", "parallel")),
        name="ffn",
    )(x, x, x, wg, wu, cw, cb, wd, lg, lb)


def _rope_tables(seq_len):
    rows_n = seq_len // GRID_W
    row = jnp.repeat(jnp.arange(rows_n, dtype=F32), GRID_W)
    col = jnp.tile(jnp.arange(GRID_W, dtype=F32), rows_n)
    half = HEAD_DIM // 2
    inv_freq = ROPE_THETA ** (-jnp.arange(0, half, 2, dtype=F32) / half)
    ang = jnp.concatenate([row[:, None] * inv_freq, col[:, None] * inv_freq], axis=-1)
    c, s = jnp.cos(ang), jnp.sin(ang)
    return jnp.concatenate([c, c, c, c], axis=-1), jnp.concatenate([-s, s, -s, s], axis=-1)


def _t5_bucket(rel):
    half = N_BUCKETS // 2
    max_exact = half // 2
    bucket = jnp.where(rel > 0, half, 0)
    rp = jnp.abs(rel)
    rpf = jnp.maximum(rp, 1).astype(F32)
    large = max_exact + (jnp.log(rpf / max_exact) / math.log(MAX_DISTANCE / max_exact)
                         * (half - max_exact)).astype(jnp.int32)
    large = jnp.minimum(large, half - 1)
    return bucket + jnp.where(rp < max_exact, rp, large)


def _deinterleave_perm(n_heads):
    per_head = np.concatenate([np.arange(0, HEAD_DIM, 2), np.arange(1, HEAD_DIM, 2)])
    return np.concatenate([h * HEAD_DIM + per_head for h in range(n_heads)])


def kernel(x, rel_bias, w_in, q_norm, k_norm, sink, out_norm_a, out_norm_b, w_out,
           ln1_g, ln1_b, w_gate, w_up, conv_w, conv_b, w_down, ln2_g, ln2_b):
    B, S, D = x.shape
    depth = w_in.shape[0]
    alpha = (2.0 * depth) ** 0.25

    cos_t, sin_t = _rope_tables(S)
    qpos = jnp.arange(W_TQ, dtype=jnp.int32)
    kpos = jnp.arange(W_TK, dtype=jnp.int32) - WINDOW
    bucket = _t5_bucket(kpos[:, None] - qpos[None, :]).astype(jnp.int32)
    blk = np.arange(LANES) // HEAD_DIM
    ones_blk = jnp.asarray(blk[:, None] == blk[None, :], dtype=BF16)

    perm_q = _deinterleave_perm(HEADS_A)
    perm_k = _deinterleave_perm(KV_A)
    col_perm = np.concatenate([perm_q, Q_A + perm_k, np.arange(Q_A + KVD_A, w_in.shape[2])])
    head_perm = _deinterleave_perm(1)

    for l in range(depth):
        w = w_in[l][:, col_perm].astype(BF16)
        gq = jnp.tile(q_norm[l][head_perm] * (SM_SCALE * LOG2E), HEADS_A)[None, :]
        gk = jnp.tile(k_norm[l][head_perm], KV_A)[None, :]
        qat, ka, vat, qbt, kb, vbt = _inproj(x, w, gq, gk, cos_t, sin_t, ones_blk, tm=512, tkc=1024, tq=256)
        ya = _gattn(qat, ka, vat)
        yb = _wattn(rel_bias, sink[l], bucket, qbt, kb, vbt)
        x = _outproj(ya, yb, x, out_norm_a[l][None, :], out_norm_b[l][None, :],
                     w_out[l][:Q_A].astype(BF16), w_out[l][Q_A:].astype(BF16),
                     ln1_g[l][None, :], ln1_b[l][None, :], alpha=alpha, tm=512)
        x = _ffn(x, w_gate[l].astype(BF16), w_up[l].astype(BF16), conv_w[l], conv_b[l][None, :],
                 w_down[l].astype(BF16), ln2_g[l][None, :], ln2_b[l][None, :],
                 alpha=alpha, tm=512, ff_chunk=256)
    return x
```

```python
import functools
import math

import jax
import jax.numpy as jnp
import numpy as np
from jax import lax
from jax.experimental import pallas as pl
from jax.experimental.pallas import tpu as pltpu

F32 = jnp.float32
BF16 = jnp.bfloat16
F8 = jnp.float8_e4m3fn
F8_SLOTS = 3
F8_SAFE_MAX = 384.0

HEAD_DIM = 64
HEADS_A = 8
HEADS_B = 8
KV_A = 2
KV_B = 2
G_A = HEADS_A // KV_A
G_B = HEADS_B // KV_B
Q_A = HEADS_A * HEAD_DIM
KVD_A = KV_A * HEAD_DIM
Q_B = HEADS_B * HEAD_DIM
KVD_B = KV_B * HEAD_DIM
GRID_W = 64
ROPE_THETA = 10000.0
Q_BLOCK = 128
WINDOW = 128
N_BUCKETS = 32
MAX_DISTANCE = 128
CONV_W = 3
RMS_EPS = 1e-6
LN_EPS = 1e-5
LOG2E = math.log2(math.e)
SM_SCALE = HEAD_DIM ** -0.5

LANES = 128
V_ROWS = 80
G_PIECE = 256
VMEM_LIMIT = 56 * 1024 * 1024


def _cparams(sem):
    return pltpu.CompilerParams(dimension_semantics=sem, vmem_limit_bytes=VMEM_LIMIT)


def _resident(shape):
    nd = len(shape)
    return pl.BlockSpec(shape, lambda *_: (0,) * nd, pipeline_mode=pl.Buffered(1))


def _split_f8(x):
    hi = x.astype(F8).astype(F32)
    lo = ((x - hi) * 4.0).astype(F8).astype(F32)
    hi4 = (hi * 0.25).astype(F8).astype(F32)
    return hi, lo, hi4


def _inproj_kernel(use_f8, x_ref, w_ref, gq_ref, gk_ref, cos_ref, sin_ref, j_ref,
                   qat_ref, ka_ref, vat_ref, qbt_ref, kb_ref, vbt_ref):
    tm = x_ref.shape[1]
    tq = qat_ref.shape[4]
    x = x_ref[0].astype(BF16)

    lane = lax.broadcasted_iota(jnp.int32, (tm, LANES), 1)
    first_half = (lane % HEAD_DIM) < (HEAD_DIM // 2)
    row = lax.broadcasted_iota(jnp.int32, (V_ROWS - HEAD_DIM, tm), 0)
    ones_rows = jnp.where(row == 0, 1.0, 0.0).astype(BF16)

    def project(c0, c1):
        return jnp.dot(x, w_ref[:, c0:c1], preferred_element_type=F32)

    def norm_rope(hc, gain):
        ssq = jnp.dot((hc * hc).astype(BF16), j_ref[...], preferred_element_type=F32)
        y = hc * lax.rsqrt(ssq * (1.0 / HEAD_DIM) + RMS_EPS) * gain
        partner = jnp.where(first_half, pltpu.roll(y, LANES - HEAD_DIM // 2, 1),
                            pltpu.roll(y, HEAD_DIM // 2, 1))
        return y * cos_ref[...] + partner * sin_ref[...]

    def finish_qa(h):
        q = jnp.concatenate(
            [norm_rope(h[:, c * LANES:(c + 1) * LANES], gq_ref[:, c * LANES:(c + 1) * LANES])
             for c in range(Q_A // LANES)], axis=1)
        qt = q.T
        if use_f8:
            qt_hi, qt_lo, qt_hi4 = _split_f8(qt)
            parts = (qt_hi, qt_hi4, qt_lo)
        else:
            parts = (qt,)
        for hh in range(HEADS_A):
            rows = slice(hh * HEAD_DIM, (hh + 1) * HEAD_DIM)
            for c in range(tm // tq):
                cols = slice(c * tq, (c + 1) * tq)
                for slot, part in enumerate(parts):
                    qat_ref[0, hh, c, slot * HEAD_DIM:(slot + 1) * HEAD_DIM, :] = (
                        part[rows, cols].astype(qat_ref.dtype))

    def finish_kva(h):
        k = norm_rope(h[:, 0:KVD_A], gk_ref[...])
        if use_f8:
            k_hi, k_lo, k_hi4 = _split_f8(k)
        vt = h[:, KVD_A:2 * KVD_A].T
        for g in range(KV_A):
            gl = slice(g * HEAD_DIM, (g + 1) * HEAD_DIM)
            if use_f8:
                ka_ref[0, g] = jnp.concatenate([k_hi[:, gl], k_lo[:, gl], k_hi4[:, gl]], axis=1).astype(F8)
            else:
                ka_ref[0, g] = k[:, gl].astype(BF16)
            vat_ref[0, g, 0, 0:HEAD_DIM, :] = vt[gl].astype(BF16)
            vat_ref[0, g, 0, HEAD_DIM:V_ROWS, :] = ones_rows

    def finish_qb(h):
        qbt = (h * (SM_SCALE * LOG2E)).T
        for hh in range(HEADS_B):
            qbt_ref[0, hh] = qbt[hh * HEAD_DIM:(hh + 1) * HEAD_DIM].astype(BF16)

    def finish_kvb(h):
        kb = h[:, 0:KVD_B]
        vbt = h[:, KVD_B:2 * KVD_B].T
        for g in range(KV_B):
            gl = slice(g * HEAD_DIM, (g + 1) * HEAD_DIM)
            kb_ref[0, g] = kb[:, gl].astype(BF16)
            vbt_ref[0, g, 0:HEAD_DIM, :] = vbt[gl].astype(BF16)
            vbt_ref[0, g, HEAD_DIM:V_ROWS, :] = ones_rows

    c_kva = Q_A
    c_qb = Q_A + 2 * KVD_A
    c_kvb = c_qb + Q_B
    h_qa = project(0, c_kva)
    h_kva = project(c_kva, c_qb)
    finish_qa(h_qa)
    h_qb = project(c_qb, c_kvb)
    finish_kva(h_kva)
    h_kvb = project(c_kvb, c_kvb + 2 * KVD_B)
    finish_qb(h_qb)
    finish_kvb(h_kvb)


def _inproj(x, w, gq, gk, cos_t, sin_t, ones_blk, *, tm, tkc, tq, use_f8):
    B, S, D = x.shape
    ns = S // tm
    n_cols = w.shape[1]
    per_chunk = tkc // tm
    qk_dim = (F8_SLOTS if use_f8 else 1) * HEAD_DIM
    qk_dtype = F8 if use_f8 else BF16
    out_shape = (
        jax.ShapeDtypeStruct((B, HEADS_A, S // tq, qk_dim, tq), qk_dtype),
        jax.ShapeDtypeStruct((B, KV_A, S, qk_dim), qk_dtype),
        jax.ShapeDtypeStruct((B, KV_A, S // tkc, V_ROWS, tkc), BF16),
        jax.ShapeDtypeStruct((B, HEADS_B, HEAD_DIM, S), BF16),
        jax.ShapeDtypeStruct((B, KV_B, S, HEAD_DIM), BF16),
        jax.ShapeDtypeStruct((B, KV_B, V_ROWS, S), BF16),
    )
    return pl.pallas_call(
        functools.partial(_inproj_kernel, use_f8),
        grid=(B, ns),
        in_specs=[
            pl.BlockSpec((1, tm, D), lambda b, s: (b, s, 0)),
            _resident((D, n_cols)),
            _resident((1, Q_A)),
            _resident((1, KVD_A)),
            pl.BlockSpec((tm, LANES), lambda b, s: (s, 0)),
            pl.BlockSpec((tm, LANES), lambda b, s: (s, 0)),
            _resident((LANES, LANES)),
        ],
        out_specs=(
            pl.BlockSpec((1, HEADS_A, tm // tq, qk_dim, tq), lambda b, s: (b, 0, s, 0, 0)),
            pl.BlockSpec((1, KV_A, tm, qk_dim), lambda b, s: (b, 0, s, 0)),
            pl.BlockSpec((1, KV_A, 1, V_ROWS, tm), lambda b, s: (b, 0, s // per_chunk, 0, s % per_chunk)),
            pl.BlockSpec((1, HEADS_B, HEAD_DIM, tm), lambda b, s: (b, 0, 0, s)),
            pl.BlockSpec((1, KV_B, tm, HEAD_DIM), lambda b, s: (b, 0, s, 0)),
            pl.BlockSpec((1, KV_B, V_ROWS, tm), lambda b, s: (b, 0, 0, s)),
        ),
        out_shape=out_shape,
        compiler_params=_cparams(("parallel", "parallel")),
        name="inproj",
    )(x, w, gq, gk, cos_t, sin_t, ones_blk)


def _gattn_kernel(qt_ref, k_ref, vt_ref, o_ref, s_sc, mnew_sc, alpha_sc, m_sc, acc_sc):
    n_chunks, _, tkc = vt_ref.shape[2:]
    nq, _, tq = qt_ref.shape[2:]
    assert n_chunks % 2 == 0 and n_chunks >= 4
    n_pieces = tkc // G_PIECE

    def reset_max():
        m_sc[...] = jnp.full(m_sc.shape, -jnp.inf, F32)

    def score_piece(qi, j, buf, h, pc):
        rows = pl.ds(pl.multiple_of(j * tkc, tkc) + pc * G_PIECE, G_PIECE)
        st = jnp.dot(k_ref[0, 0, rows, :], qt_ref[0, h, qi], preferred_element_type=F32)
        s_sc[buf, h, pc * G_PIECE:(pc + 1) * G_PIECE, :] = st
        return jnp.max(st, axis=0, keepdims=True)

    def value_piece(j, buf, h, pc):
        cols = slice(pc * G_PIECE, (pc + 1) * G_PIECE)
        pt = jnp.exp2(s_sc[buf, h, cols, :] - mnew_sc[buf, h]).astype(BF16)
        return jnp.dot(vt_ref[0, 0, j, :, cols], pt, preferred_element_type=F32)

    def stage(score_of, value_of):
        for h in range(G_A):
            piece_max, pv = [], None
            for pc in range(n_pieces):
                if score_of is not None:
                    piece_max.append(score_piece(*score_of, h, pc))
                if value_of is not None:
                    d = value_piece(*value_of, h, pc)
                    pv = d if pv is None else pv + d
            if score_of is not None:
                buf = score_of[2]
                m_prev = m_sc[h]
                m_new = functools.reduce(jnp.maximum, piece_max, m_prev)
                alpha_sc[buf, h] = jnp.exp2(m_prev - m_new)
                mnew_sc[buf, h] = m_new
                m_sc[h] = m_new
            if value_of is not None:
                acc_sc[h] = alpha_sc[value_of[1], h] * acc_sc[h] + pv

    def middle_chunks(qi):
        def pair(i, carry):
            stage((qi, 2 * i + 1, 1), (2 * i, 0))
            stage((qi, 2 * i + 2, 0), (2 * i + 1, 1))
            return carry

        lax.fori_loop(0, n_chunks // 2 - 1, pair, 0)
        stage((qi, n_chunks - 1, 1), (n_chunks - 2, 0))

    def finalize(qi):
        outs = []
        for h in range(G_A):
            a = acc_sc[h]
            outs.append(a[0:HEAD_DIM] / a[HEAD_DIM:HEAD_DIM + 1])
        rows = pl.ds(pl.multiple_of(qi * tq, tq), tq)
        o_ref[0, rows, :] = jnp.concatenate(outs, axis=0).T.astype(o_ref.dtype)

    acc_sc[...] = jnp.zeros(acc_sc.shape, F32)
    reset_max()
    stage((0, 0, 0), None)

    def q_block(qi, carry):
        middle_chunks(qi)
        reset_max()
        stage((qi + 1, 0, 0), (n_chunks - 1, 1))
        finalize(qi)
        return carry

    lax.fori_loop(0, nq - 1, q_block, 0)
    middle_chunks(nq - 1)
    stage(None, (n_chunks - 1, 1))
    finalize(nq - 1)


def _gattn(qat, ka, vat):
    B, _, nq, qk_dim, tq = qat.shape
    S = nq * tq
    n_chunks, _, tkc = vat.shape[2:]
    return pl.pallas_call(
        _gattn_kernel,
        grid=(B, KV_A),
        in_specs=[
            pl.BlockSpec((1, G_A, nq, qk_dim, tq), lambda b, g: (b, g, 0, 0, 0)),
            pl.BlockSpec((1, 1, S, qk_dim), lambda b, g: (b, g, 0, 0)),
            pl.BlockSpec((1, 1, n_chunks, V_ROWS, tkc), lambda b, g: (b, g, 0, 0, 0)),
        ],
        out_specs=pl.BlockSpec((1, S, G_A * HEAD_DIM), lambda b, g: (b, 0, g)),
        out_shape=jax.ShapeDtypeStruct((B, S, Q_A), BF16),
        scratch_shapes=[
            pltpu.VMEM((2, G_A, tkc, tq), F32),
            pltpu.VMEM((2, G_A, 1, tq), F32),
            pltpu.VMEM((2, G_A, 1, tq), F32),
            pltpu.VMEM((G_A, 1, tq), F32),
            pltpu.VMEM((G_A, V_ROWS, tq), F32),
        ],
        compiler_params=_cparams(("parallel", "parallel")),
        name="gattn",
    )(qat, ka, vat)


W_TQ = 2 * Q_BLOCK
W_TK = W_TQ + 2 * WINDOW
W_NBLK = W_TK // Q_BLOCK


def _wattn_kernel(rb_ref, sink_ref, bucket_ref, qt_ref, *refs):
    k_refs = refs[:W_NBLK]
    v_refs = refs[W_NBLK:2 * W_NBLK]
    o_ref, bias_sc, s_sc = refs[2 * W_NBLK:]
    b = pl.program_id(0)
    i = pl.program_id(1)
    n = pl.num_programs(1)

    @pl.when((b == 0) & (i == 0))
    def _():
        bucket = bucket_ref[...]
        krow = lax.broadcasted_iota(jnp.int32, (W_TK, W_TQ), 0)
        qcol = lax.broadcasted_iota(jnp.int32, (W_TK, W_TQ), 1)
        in_window = jnp.abs(krow - WINDOW - qcol) <= WINDOW
        for h in range(HEADS_B):
            t = jnp.zeros((W_TK, W_TQ), F32)
            for bk in range(N_BUCKETS):
                t = jnp.where(bucket == bk, rb_ref[bk, h], t)
            t = jnp.where(in_window, t * LOG2E, -jnp.inf)
            bias_sc[0, h] = jnp.where(krow >= Q_BLOCK, t, -jnp.inf)
            bias_sc[1, h] = t
            bias_sc[2, h] = jnp.where(krow < W_TK - Q_BLOCK, t, -jnp.inf)

    table = jnp.where(i == 0, 0, jnp.where(i == n - 1, 2, 1))
    kws = [jnp.concatenate([r[0, g] for r in k_refs], axis=0) for g in range(KV_B)]
    vws = [jnp.concatenate([r[0, g] for r in v_refs], axis=1) for g in range(KV_B)]
    sinks = [sink_ref[h] * LOG2E for h in range(HEADS_B)]
    maxes = [None] * HEADS_B

    def scores(h):
        st = jnp.dot(kws[h // G_B], qt_ref[0, h], preferred_element_type=F32) + bias_sc[table, h]
        s_sc[h] = st
        maxes[h] = jnp.maximum(jnp.max(st, axis=0, keepdims=True), sinks[h])

    def finish(h):
        pt = jnp.exp2(s_sc[h] - maxes[h]).astype(BF16)
        acc = jnp.dot(vws[h // G_B], pt, preferred_element_type=F32)
        denom = acc[HEAD_DIM:HEAD_DIM + 1] + jnp.exp2(sinks[h] - maxes[h])
        return acc[0:HEAD_DIM] / denom

    ahead = 2
    for h in range(ahead):
        scores(h)
    outs = []
    for h in range(HEADS_B):
        if h + ahead < HEADS_B:
            scores(h + ahead)
        outs.append(finish(h))
    o_ref[0] = jnp.concatenate(outs, axis=0).T.astype(o_ref.dtype)


def _wattn(rel_bias, sink, bucket, qbt, kb, vbt):
    B, _, _, S = qbt.shape
    n = S // W_TQ
    last_blk = S // Q_BLOCK - 1
    smem = pl.BlockSpec(memory_space=pltpu.SMEM)

    def blk(c):
        return lambda b, i: jnp.clip(i * (W_TQ // Q_BLOCK) - 1 + c, 0, last_blk)

    k_specs = [pl.BlockSpec((1, KV_B, Q_BLOCK, HEAD_DIM), lambda b, i, f=blk(c): (b, 0, f(b, i), 0))
               for c in range(W_NBLK)]
    v_specs = [pl.BlockSpec((1, KV_B, V_ROWS, Q_BLOCK), lambda b, i, f=blk(c): (b, 0, 0, f(b, i)))
               for c in range(W_NBLK)]
    return pl.pallas_call(
        _wattn_kernel,
        grid=(B, n),
        in_specs=[smem, smem, _resident((W_TK, W_TQ)),
                  pl.BlockSpec((1, HEADS_B, HEAD_DIM, W_TQ), lambda b, i: (b, 0, 0, i))] + k_specs + v_specs,
        out_specs=pl.BlockSpec((1, W_TQ, Q_B), lambda b, i: (b, i, 0)),
        out_shape=jax.ShapeDtypeStruct((B, S, Q_B), BF16),
        scratch_shapes=[pltpu.VMEM((3, HEADS_B, W_TK, W_TQ), F32), pltpu.VMEM((HEADS_B, W_TK, W_TQ), F32)],
        compiler_params=_cparams(("arbitrary", "arbitrary")),
        name="wattn",
    )(rel_bias, sink, bucket, qbt, *([kb] * W_NBLK), *([vbt] * W_NBLK))


def _layer_norm(y, g, b):
    mu = jnp.mean(y, axis=-1, keepdims=True)
    d = y - mu
    var = jnp.mean(d * d, axis=-1, keepdims=True)
    return d * lax.rsqrt(var + LN_EPS) * g + b


def _rms(y, g):
    yf = y.astype(F32)
    return yf * lax.rsqrt(jnp.mean(yf * yf, axis=-1, keepdims=True) + RMS_EPS) * g


def _outproj_kernel(alpha, ya_ref, yb_ref, x_ref, ga_ref, gb_ref, wa_ref, wb_ref, lg_ref, lb_ref, o_ref):
    na = _rms(ya_ref[0], ga_ref[...]).astype(BF16)
    nb = _rms(yb_ref[0], gb_ref[...]).astype(BF16)
    mix = (jnp.dot(na, wa_ref[...], preferred_element_type=F32)
           + jnp.dot(nb, wb_ref[...], preferred_element_type=F32))
    o_ref[0] = _layer_norm(alpha * x_ref[0] + mix, lg_ref[...], lb_ref[...])


def _outproj(ya, yb, x, ga, gb, wa, wb, lg, lb, *, alpha, tm):
    B, S, D = x.shape
    row = lambda w: pl.BlockSpec((1, tm, w), lambda b, s: (b, s, 0))
    return pl.pallas_call(
        functools.partial(_outproj_kernel, alpha),
        grid=(B, S // tm),
        in_specs=[row(Q_A), row(Q_B), row(D), _resident((1, Q_A)), _resident((1, Q_B)),
                  _resident((Q_A, D)), _resident((Q_B, D)), _resident((1, D)), _resident((1, D))],
        out_specs=row(D),
        out_shape=jax.ShapeDtypeStruct((B, S, D), F32),
        compiler_params=_cparams(("parallel", "parallel")),
        name="outproj",
    )(ya, yb, x, ga, gb, wa, wb, lg, lb)


def _gelu_tanh(x):
    return 0.5 * x * (1.0 + jnp.tanh(math.sqrt(2.0 / math.pi) * (x + 0.044715 * (x * x * x))))


def _ffn_kernel(alpha, ff_chunk, x_ref, xp_ref, xn_ref, wg_ref, wu_ref, cw_ref, cb_ref, wd_ref,
                lg_ref, lb_ref, o_ref, h_sc):
    s = pl.program_id(1)
    ns = pl.num_programs(1)
    tm = x_ref.shape[1]
    d_ff = wg_ref.shape[1]
    halo = xp_ref.shape[1]
    x = x_ref[0]
    xb = x.astype(BF16)
    xh = jnp.concatenate([xp_ref[0], xn_ref[0]], axis=0).astype(BF16)
    has_prev = (s > 0).astype(F32)
    has_next = (s < ns - 1).astype(F32)
    row = lax.broadcasted_iota(jnp.int32, (tm, 1), 0)
    for c in range(d_ff // ff_chunk):
        cs = slice(c * ff_chunk, (c + 1) * ff_chunk)
        wg = wg_ref[:, cs]
        g = jnp.dot(xb, wg, preferred_element_type=F32)
        u = jnp.dot(xb, wu_ref[:, cs], preferred_element_type=F32)
        gh = jnp.dot(xh, wg, preferred_element_type=F32)
        g_before = gh[halo - 1:halo] * has_prev
        g_after = gh[halo:halo + 1] * has_next
        g_m1 = jnp.where(row == 0, g_before, pltpu.roll(g, 1, 0))
        g_p1 = jnp.where(row == tm - 1, g_after, pltpu.roll(g, tm - 1, 0))
        gc = cb_ref[:, cs] + g_m1 * cw_ref[0:1, cs] + g * cw_ref[1:2, cs] + g_p1 * cw_ref[2:3, cs]
        h_sc[:, cs] = (_gelu_tanh(gc) * u).astype(BF16)
    ffn = jnp.dot(h_sc[...], wd_ref[...], preferred_element_type=F32)
    o_ref[0] = _layer_norm(alpha * x + ffn, lg_ref[...], lb_ref[...])


def _ffn(x, wg, wu, cw, cb, wd, lg, lb, *, alpha, tm, ff_chunk, halo=8):
    B, S, D = x.shape
    d_ff = wg.shape[1]
    ns = S // tm
    hb = tm // halo
    last = S // halo - 1
    return pl.pallas_call(
        functools.partial(_ffn_kernel, alpha, ff_chunk),
        grid=(B, ns),
        in_specs=[
            pl.BlockSpec((1, tm, D), lambda b, s: (b, s, 0)),
            pl.BlockSpec((1, halo, D), lambda b, s: (b, jnp.maximum(s * hb - 1, 0), 0)),
            pl.BlockSpec((1, halo, D), lambda b, s: (b, jnp.minimum((s + 1) * hb, last), 0)),
            _resident((D, d_ff)), _resident((D, d_ff)), _resident((CONV_W, d_ff)), _resident((1, d_ff)),
            _resident((d_ff, D)), _resident((1, D)), _resident((1, D)),
        ],
        out_specs=pl.BlockSpec((1, tm, D), lambda b, s: (b, s, 0)),
        out_shape=jax.ShapeDtypeStruct((B, S, D), F32),
        scratch_shapes=[pltpu.VMEM((tm, d_ff), BF16)],
        compiler_params=_cparams(("parallel", "parallel")),
        name="ffn",
    )(x, x, x, wg, wu, cw, cb, wd, lg, lb)


def _rope_tables(seq_len):
    rows_n = seq_len // GRID_W
    row = jnp.repeat(jnp.arange(rows_n, dtype=F32), GRID_W)
    col = jnp.tile(jnp.arange(GRID_W, dtype=F32), rows_n)
    half = HEAD_DIM // 2
    inv_freq = ROPE_THETA ** (-jnp.arange(0, half, 2, dtype=F32) / half)
    ang = jnp.concatenate([row[:, None] * inv_freq, col[:, None] * inv_freq], axis=-1)
    c, s = jnp.cos(ang), jnp.sin(ang)
    return jnp.concatenate([c, c, c, c], axis=-1), jnp.concatenate([-s, s, -s, s], axis=-1)


def _t5_bucket(rel):
    half = N_BUCKETS // 2
    max_exact = half // 2
    bucket = jnp.where(rel > 0, half, 0)
    rp = jnp.abs(rel)
    rpf = jnp.maximum(rp, 1).astype(F32)
    large = max_exact + (jnp.log(rpf / max_exact) / math.log(MAX_DISTANCE / max_exact)
                         * (half - max_exact)).astype(jnp.int32)
    large = jnp.minimum(large, half - 1)
    return bucket + jnp.where(rp < max_exact, rp, large)


def _deinterleave_perm(n_heads):
    per_head = np.concatenate([np.arange(0, HEAD_DIM, 2), np.arange(1, HEAD_DIM, 2)])
    return np.concatenate([h * HEAD_DIM + per_head for h in range(n_heads)])


def kernel(x, rel_bias, w_in, q_norm, k_norm, sink, out_norm_a, out_norm_b, w_out,
           ln1_g, ln1_b, w_gate, w_up, conv_w, conv_b, w_down, ln2_g, ln2_b):
    B, S, D = x.shape
    depth = w_in.shape[0]
    alpha = (2.0 * depth) ** 0.25

    cos_t, sin_t = _rope_tables(S)
    qpos = jnp.arange(W_TQ, dtype=jnp.int32)
    kpos = jnp.arange(W_TK, dtype=jnp.int32) - WINDOW
    bucket = _t5_bucket(kpos[:, None] - qpos[None, :]).astype(jnp.int32)
    blk = np.arange(LANES) // HEAD_DIM
    ones_blk = jnp.asarray(blk[:, None] == blk[None, :], dtype=BF16)

    perm_q = _deinterleave_perm(HEADS_A)
    perm_k = _deinterleave_perm(KV_A)
    col_perm = np.concatenate([perm_q, Q_A + perm_k, np.arange(Q_A + KVD_A, w_in.shape[2])])
    head_perm = _deinterleave_perm(1)

    for l in range(depth):
        w = w_in[l][:, col_perm].astype(BF16)
        gq_max = jnp.max(jnp.abs(q_norm[l])) * (SM_SCALE * LOG2E)
        gk_max = jnp.max(jnp.abs(k_norm[l]))
        bal = jnp.exp2(jnp.round(0.5 * jnp.log2(jnp.maximum(gq_max, 1e-30) / jnp.maximum(gk_max, 1e-30))))
        gq = jnp.tile(q_norm[l][head_perm] * (SM_SCALE * LOG2E) / bal, HEADS_A)[None, :]
        gk = jnp.tile(k_norm[l][head_perm] * bal, KV_A)[None, :]

        def front(use_f8, x, w, gq, gk):
            qat, ka, vat, qbt, kb, vbt = _inproj(x, w, gq, gk, cos_t, sin_t, ones_blk,
                                                 tm=512, tkc=1024, tq=256, use_f8=use_f8)
            return _gattn(qat, ka, vat), qbt, kb, vbt

        bound = math.sqrt(HEAD_DIM) * jnp.maximum(gk_max * bal, gq_max / bal)
        ya, qbt, kb, vbt = lax.cond(bound <= F8_SAFE_MAX, functools.partial(front, True),
                                    functools.partial(front, False), x, w, gq, gk)
        yb = _wattn(rel_bias, sink[l], bucket, qbt, kb, vbt)
        x = _outproj(ya, yb, x, out_norm_a[l][None, :], out_norm_b[l][None, :],
                     w_out[l][:Q_A].astype(BF16), w_out[l][Q_A:].astype(BF16),
                     ln1_g[l][None, :], ln1_b[l][None, :], alpha=alpha, tm=512)
        x = _ffn(x, w_gate[l].astype(BF16), w_up[l].astype(BF16), conv_w[l], conv_b[l][None, :],
                 w_down[l].astype(BF16), ln2_g[l][None, :], ln2_b[l][None, :],
                 alpha=alpha, tm=512, ff_chunk=256)
    return x
```

```python
import functools
import math

import jax
import jax.numpy as jnp
import numpy as np
from jax import lax
from jax.experimental import pallas as pl
from jax.experimental.pallas import tpu as pltpu

F32 = jnp.float32
BF16 = jnp.bfloat16
F8 = jnp.float8_e4m3fn
F8_SLOTS = 3
F8_SAFE_MAX = 384.0

HEAD_DIM = 64
HEADS_A = 8
HEADS_B = 8
KV_A = 2
KV_B = 2
G_A = HEADS_A // KV_A
G_B = HEADS_B // KV_B
Q_A = HEADS_A * HEAD_DIM
KVD_A = KV_A * HEAD_DIM
Q_B = HEADS_B * HEAD_DIM
KVD_B = KV_B * HEAD_DIM
GRID_W = 64
ROPE_THETA = 10000.0
Q_BLOCK = 128
WINDOW = 128
N_BUCKETS = 32
MAX_DISTANCE = 128
CONV_W = 3
RMS_EPS = 1e-6
LN_EPS = 1e-5
LOG2E = math.log2(math.e)
SM_SCALE = HEAD_DIM ** -0.5

LANES = 128
V_ROWS = 80
G_PIECE = 256
G_SCORE_PIECES = 2
VMEM_LIMIT = 56 * 1024 * 1024


def _cparams(sem):
    return pltpu.CompilerParams(dimension_semantics=sem, vmem_limit_bytes=VMEM_LIMIT)


def _resident(shape):
    nd = len(shape)
    return pl.BlockSpec(shape, lambda *_: (0,) * nd, pipeline_mode=pl.Buffered(1))


def _split_f8(x):
    hi = x.astype(F8).astype(F32)
    lo = ((x - hi) * 4.0).astype(F8).astype(F32)
    hi4 = (hi * 0.25).astype(F8).astype(F32)
    return hi, lo, hi4


def _inproj_kernel(use_f8, x_ref, w_ref, gq_ref, gk_ref, cos_ref, sin_ref, j_ref,
                   qat_ref, ka_ref, vat_ref, qbt_ref, kb_ref, vbt_ref):
    tm = x_ref.shape[1]
    tq = qat_ref.shape[4]
    x = x_ref[0].astype(BF16)

    lane = lax.broadcasted_iota(jnp.int32, (tm, LANES), 1)
    first_half = (lane % HEAD_DIM) < (HEAD_DIM // 2)
    row = lax.broadcasted_iota(jnp.int32, (V_ROWS - HEAD_DIM, tm), 0)
    ones_rows = jnp.where(row == 0, 1.0, 0.0).astype(BF16)

    def project(c0, c1):
        return jnp.dot(x, w_ref[:, c0:c1], preferred_element_type=F32)

    def norm_rope(hc, gain):
        ssq = jnp.dot((hc * hc).astype(BF16), j_ref[...], preferred_element_type=F32)
        y = hc * lax.rsqrt(ssq * (1.0 / HEAD_DIM) + RMS_EPS) * gain
        partner = jnp.where(first_half, pltpu.roll(y, LANES - HEAD_DIM // 2, 1),
                            pltpu.roll(y, HEAD_DIM // 2, 1))
        return y * cos_ref[...] + partner * sin_ref[...]

    def finish_qa(h):
        q = jnp.concatenate(
            [norm_rope(h[:, c * LANES:(c + 1) * LANES], gq_ref[:, c * LANES:(c + 1) * LANES])
             for c in range(Q_A // LANES)], axis=1)
        qt = q.T
        if use_f8:
            qt_hi, qt_lo, qt_hi4 = _split_f8(qt)
            parts = (qt_hi, qt_hi4, qt_lo)
        else:
            parts = (qt,)
        for hh in range(HEADS_A):
            rows = slice(hh * HEAD_DIM, (hh + 1) * HEAD_DIM)
            for c in range(tm // tq):
                cols = slice(c * tq, (c + 1) * tq)
                for slot, part in enumerate(parts):
                    qat_ref[0, hh, c, slot * HEAD_DIM:(slot + 1) * HEAD_DIM, :] = (
                        part[rows, cols].astype(qat_ref.dtype))

    def finish_kva(h):
        k = norm_rope(h[:, 0:KVD_A], gk_ref[...])
        if use_f8:
            k_hi, k_lo, k_hi4 = _split_f8(k)
        vt = h[:, KVD_A:2 * KVD_A].T
        for g in range(KV_A):
            gl = slice(g * HEAD_DIM, (g + 1) * HEAD_DIM)
            if use_f8:
                ka_ref[0, g] = jnp.concatenate([k_hi[:, gl], k_lo[:, gl], k_hi4[:, gl]], axis=1).astype(F8)
            else:
                ka_ref[0, g] = k[:, gl].astype(BF16)
            vat_ref[0, g, 0, 0:HEAD_DIM, :] = vt[gl].astype(BF16)
            vat_ref[0, g, 0, HEAD_DIM:V_ROWS, :] = ones_rows

    def finish_qb(h):
        qbt = (h * (SM_SCALE * LOG2E)).T
        for hh in range(HEADS_B):
            qbt_ref[0, hh] = qbt[hh * HEAD_DIM:(hh + 1) * HEAD_DIM].astype(BF16)

    def finish_kvb(h):
        kb = h[:, 0:KVD_B]
        vbt = h[:, KVD_B:2 * KVD_B].T
        for g in range(KV_B):
            gl = slice(g * HEAD_DIM, (g + 1) * HEAD_DIM)
            kb_ref[0, g] = kb[:, gl].astype(BF16)
            vbt_ref[0, g, 0:HEAD_DIM, :] = vbt[gl].astype(BF16)
            vbt_ref[0, g, HEAD_DIM:V_ROWS, :] = ones_rows

    c_kva = Q_A
    c_qb = Q_A + 2 * KVD_A
    c_kvb = c_qb + Q_B
    h_qa = project(0, c_kva)
    h_kva = project(c_kva, c_qb)
    finish_qa(h_qa)
    h_qb = project(c_qb, c_kvb)
    finish_kva(h_kva)
    h_kvb = project(c_kvb, c_kvb + 2 * KVD_B)
    finish_qb(h_qb)
    finish_kvb(h_kvb)


def _inproj(x, w, gq, gk, cos_t, sin_t, ones_blk, *, tm, tkc, tq, use_f8):
    B, S, D = x.shape
    ns = S // tm
    n_cols = w.shape[1]
    per_chunk = tkc // tm
    qk_dim = (F8_SLOTS if use_f8 else 1) * HEAD_DIM
    qk_dtype = F8 if use_f8 else BF16
    out_shape = (
        jax.ShapeDtypeStruct((B, HEADS_A, S // tq, qk_dim, tq), qk_dtype),
        jax.ShapeDtypeStruct((B, KV_A, S, qk_dim), qk_dtype),
        jax.ShapeDtypeStruct((B, KV_A, S // tkc, V_ROWS, tkc), BF16),
        jax.ShapeDtypeStruct((B, HEADS_B, HEAD_DIM, S), BF16),
        jax.ShapeDtypeStruct((B, KV_B, S, HEAD_DIM), BF16),
        jax.ShapeDtypeStruct((B, KV_B, V_ROWS, S), BF16),
    )
    return pl.pallas_call(
        functools.partial(_inproj_kernel, use_f8),
        grid=(B, ns),
        in_specs=[
            pl.BlockSpec((1, tm, D), lambda b, s: (b, s, 0)),
            _resident((D, n_cols)),
            _resident((1, Q_A)),
            _resident((1, KVD_A)),
            pl.BlockSpec((tm, LANES), lambda b, s: (s, 0)),
            pl.BlockSpec((tm, LANES), lambda b, s: (s, 0)),
            _resident((LANES, LANES)),
        ],
        out_specs=(
            pl.BlockSpec((1, HEADS_A, tm // tq, qk_dim, tq), lambda b, s: (b, 0, s, 0, 0)),
            pl.BlockSpec((1, KV_A, tm, qk_dim), lambda b, s: (b, 0, s, 0)),
            pl.BlockSpec((1, KV_A, 1, V_ROWS, tm), lambda b, s: (b, 0, s // per_chunk, 0, s % per_chunk)),
            pl.BlockSpec((1, HEADS_B, HEAD_DIM, tm), lambda b, s: (b, 0, 0, s)),
            pl.BlockSpec((1, KV_B, tm, HEAD_DIM), lambda b, s: (b, 0, s, 0)),
            pl.BlockSpec((1, KV_B, V_ROWS, tm), lambda b, s: (b, 0, 0, s)),
        ),
        out_shape=out_shape,
        compiler_params=_cparams(("parallel", "parallel")),
        name="inproj",
    )(x, w, gq, gk, cos_t, sin_t, ones_blk)


def _gattn_kernel(qt_ref, k_ref, vt_ref, o_ref, s_sc, mnew_sc, alpha_sc, m_sc, acc_sc):
    n_chunks, _, tkc = vt_ref.shape[2:]
    nq, _, tq = qt_ref.shape[2:]
    assert n_chunks % 2 == 0 and n_chunks >= 4
    n_pieces = tkc // G_PIECE

    def reset_max():
        m_sc[...] = jnp.full(m_sc.shape, -jnp.inf, F32)

    def score_piece(qi, j, buf, h, pc):
        rows = pl.ds(pl.multiple_of(j * tkc, tkc) + pc * G_PIECE, G_SCORE_PIECES * G_PIECE)
        st = jnp.dot(k_ref[0, 0, rows, :], qt_ref[0, h, qi], preferred_element_type=F32)
        s_sc[buf, h, pc * G_PIECE:(pc + G_SCORE_PIECES) * G_PIECE, :] = st
        return jnp.max(st, axis=0, keepdims=True)

    def value_piece(j, buf, h, pc):
        cols = slice(pc * G_PIECE, (pc + 1) * G_PIECE)
        pt = jnp.exp2(s_sc[buf, h, cols, :] - mnew_sc[buf, h]).astype(BF16)
        return jnp.dot(vt_ref[0, 0, j, :, cols], pt, preferred_element_type=F32)

    def stage(score_of, value_of):
        for h in range(G_A):
            piece_max, pv = [], None
            for pc in range(n_pieces):
                if score_of is not None and pc % G_SCORE_PIECES == 0:
                    piece_max.append(score_piece(*score_of, h, pc))
                if value_of is not None:
                    d = value_piece(*value_of, h, pc)
                    pv = d if pv is None else pv + d
            if score_of is not None:
                buf = score_of[2]
                m_prev = m_sc[h]
                m_new = functools.reduce(jnp.maximum, piece_max, m_prev)
                alpha_sc[buf, h] = jnp.exp2(m_prev - m_new)
                mnew_sc[buf, h] = m_new
                m_sc[h] = m_new
            if value_of is not None:
                acc_sc[h] = alpha_sc[value_of[1], h] * acc_sc[h] + pv

    def middle_chunks(qi):
        def pair(i, carry):
            stage((qi, 2 * i + 1, 1), (2 * i, 0))
            stage((qi, 2 * i + 2, 0), (2 * i + 1, 1))
            return carry

        lax.fori_loop(0, n_chunks // 2 - 1, pair, 0, unroll=True)
        stage((qi, n_chunks - 1, 1), (n_chunks - 2, 0))

    def finalize(qi):
        outs = []
        for h in range(G_A):
            a = acc_sc[h]
            outs.append(a[0:HEAD_DIM] / a[HEAD_DIM:HEAD_DIM + 1])
        rows = pl.ds(pl.multiple_of(qi * tq, tq), tq)
        o_ref[0, rows, :] = jnp.concatenate(outs, axis=0).T.astype(o_ref.dtype)

    acc_sc[...] = jnp.zeros(acc_sc.shape, F32)
    reset_max()
    stage((0, 0, 0), None)

    def q_block(qi, carry):
        middle_chunks(qi)
        reset_max()
        stage((qi + 1, 0, 0), (n_chunks - 1, 1))
        finalize(qi)
        return carry

    lax.fori_loop(0, nq - 1, q_block, 0)
    middle_chunks(nq - 1)
    stage(None, (n_chunks - 1, 1))
    finalize(nq - 1)


def _gattn(qat, ka, vat):
    B, _, nq, qk_dim, tq = qat.shape
    S = nq * tq
    n_chunks, _, tkc = vat.shape[2:]
    return pl.pallas_call(
        _gattn_kernel,
        grid=(B, KV_A),
        in_specs=[
            pl.BlockSpec((1, G_A, nq, qk_dim, tq), lambda b, g: (b, g, 0, 0, 0)),
            pl.BlockSpec((1, 1, S, qk_dim), lambda b, g: (b, g, 0, 0)),
            pl.BlockSpec((1, 1, n_chunks, V_ROWS, tkc), lambda b, g: (b, g, 0, 0, 0)),
        ],
        out_specs=pl.BlockSpec((1, S, G_A * HEAD_DIM), lambda b, g: (b, 0, g)),
        out_shape=jax.ShapeDtypeStruct((B, S, Q_A), BF16),
        scratch_shapes=[
            pltpu.VMEM((2, G_A, tkc, tq), F32),
            pltpu.VMEM((2, G_A, 1, tq), F32),
            pltpu.VMEM((2, G_A, 1, tq), F32),
            pltpu.VMEM((G_A, 1, tq), F32),
            pltpu.VMEM((G_A, V_ROWS, tq), F32),
        ],
        compiler_params=_cparams(("parallel", "parallel")),
        name="gattn",
    )(qat, ka, vat)


W_TQ = 2 * Q_BLOCK
W_TK = W_TQ + 2 * WINDOW
W_NBLK = W_TK // Q_BLOCK


def _wattn_kernel(rb_ref, sink_ref, bucket_ref, qt_ref, *refs):
    k_refs = refs[:W_NBLK]
    v_refs = refs[W_NBLK:2 * W_NBLK]
    o_ref, bias_sc, s_sc = refs[2 * W_NBLK:]
    b = pl.program_id(0)
    i = pl.program_id(1)
    n = pl.num_programs(1)

    @pl.when((b == 0) & (i == 0))
    def _():
        bucket = bucket_ref[...]
        krow = lax.broadcasted_iota(jnp.int32, (W_TK, W_TQ), 0)
        qcol = lax.broadcasted_iota(jnp.int32, (W_TK, W_TQ), 1)
        in_window = jnp.abs(krow - WINDOW - qcol) <= WINDOW
        for h in range(HEADS_B):
            t = jnp.zeros((W_TK, W_TQ), F32)
            for bk in range(N_BUCKETS):
                t = jnp.where(bucket == bk, rb_ref[bk, h], t)
            t = jnp.where(in_window, t * LOG2E, -jnp.inf)
            bias_sc[0, h] = jnp.where(krow >= Q_BLOCK, t, -jnp.inf)
            bias_sc[1, h] = t
            bias_sc[2, h] = jnp.where(krow < W_TK - Q_BLOCK, t, -jnp.inf)

    table = jnp.where(i == 0, 0, jnp.where(i == n - 1, 2, 1))
    kws = [jnp.concatenate([r[0, g] for r in k_refs], axis=0) for g in range(KV_B)]
    vws = [jnp.concatenate([r[0, g] for r in v_refs], axis=1) for g in range(KV_B)]
    sinks = [sink_ref[h] * LOG2E for h in range(HEADS_B)]
    maxes = [None] * HEADS_B

    def scores(h):
        st = jnp.dot(kws[h // G_B], qt_ref[0, h], preferred_element_type=F32) + bias_sc[table, h]
        s_sc[h] = st
        maxes[h] = jnp.maximum(jnp.max(st, axis=0, keepdims=True), sinks[h])

    def finish(h):
        pt = jnp.exp2(s_sc[h] - maxes[h]).astype(BF16)
        acc = jnp.dot(vws[h // G_B], pt, preferred_element_type=F32)
        denom = acc[HEAD_DIM:HEAD_DIM + 1] + jnp.exp2(sinks[h] - maxes[h])
        return acc[0:HEAD_DIM] / denom

    ahead = 2
    for h in range(ahead):
        scores(h)
    outs = []
    for h in range(HEADS_B):
        if h + ahead < HEADS_B:
            scores(h + ahead)
        outs.append(finish(h))
    o_ref[0] = jnp.concatenate(outs, axis=0).T.astype(o_ref.dtype)


def _wattn(rel_bias, sink, bucket, qbt, kb, vbt):
    B, _, _, S = qbt.shape
    n = S // W_TQ
    last_blk = S // Q_BLOCK - 1
    smem = pl.BlockSpec(memory_space=pltpu.SMEM)

    def blk(c):
        return lambda b, i: jnp.clip(i * (W_TQ // Q_BLOCK) - 1 + c, 0, last_blk)

    k_specs = [pl.BlockSpec((1, KV_B, Q_BLOCK, HEAD_DIM), lambda b, i, f=blk(c): (b, 0, f(b, i), 0))
               for c in range(W_NBLK)]
    v_specs = [pl.BlockSpec((1, KV_B, V_ROWS, Q_BLOCK), lambda b, i, f=blk(c): (b, 0, 0, f(b, i)))
               for c in range(W_NBLK)]
    return pl.pallas_call(
        _wattn_kernel,
        grid=(B, n),
        in_specs=[smem, smem, _resident((W_TK, W_TQ)),
                  pl.BlockSpec((1, HEADS_B, HEAD_DIM, W_TQ), lambda b, i: (b, 0, 0, i))] + k_specs + v_specs,
        out_specs=pl.BlockSpec((1, W_TQ, Q_B), lambda b, i: (b, i, 0)),
        out_shape=jax.ShapeDtypeStruct((B, S, Q_B), BF16),
        scratch_shapes=[pltpu.VMEM((3, HEADS_B, W_TK, W_TQ), F32), pltpu.VMEM((HEADS_B, W_TK, W_TQ), F32)],
        compiler_params=_cparams(("arbitrary", "arbitrary")),
        name="wattn",
    )(rel_bias, sink, bucket, qbt, *([kb] * W_NBLK), *([vbt] * W_NBLK))


def _layer_norm(y, g, b):
    mu = jnp.mean(y, axis=-1, keepdims=True)
    d = y - mu
    var = jnp.mean(d * d, axis=-1, keepdims=True)
    return d * lax.rsqrt(var + LN_EPS) * g + b


def _rms(y, g):
    yf = y.astype(F32)
    return yf * lax.rsqrt(jnp.mean(yf * yf, axis=-1, keepdims=True) + RMS_EPS) * g


def _outproj_kernel(alpha, ya_ref, yb_ref, x_ref, ga_ref, gb_ref, wa_ref, wb_ref, lg_ref, lb_ref, o_ref):
    na = _rms(ya_ref[0], ga_ref[...]).astype(BF16)
    nb = _rms(yb_ref[0], gb_ref[...]).astype(BF16)
    mix = (jnp.dot(na, wa_ref[...], preferred_element_type=F32)
           + jnp.dot(nb, wb_ref[...], preferred_element_type=F32))
    o_ref[0] = _layer_norm(alpha * x_ref[0] + mix, lg_ref[...], lb_ref[...])


def _outproj(ya, yb, x, ga, gb, wa, wb, lg, lb, *, alpha, tm):
    B, S, D = x.shape
    row = lambda w: pl.BlockSpec((1, tm, w), lambda b, s: (b, s, 0))
    return pl.pallas_call(
        functools.partial(_outproj_kernel, alpha),
        grid=(B, S // tm),
        in_specs=[row(Q_A), row(Q_B), row(D), _resident((1, Q_A)), _resident((1, Q_B)),
                  _resident((Q_A, D)), _resident((Q_B, D)), _resident((1, D)), _resident((1, D))],
        out_specs=row(D),
        out_shape=jax.ShapeDtypeStruct((B, S, D), F32),
        compiler_params=_cparams(("parallel", "parallel")),
        name="outproj",
    )(ya, yb, x, ga, gb, wa, wb, lg, lb)


def _gelu_tanh(x):
    return 0.5 * x * (1.0 + jnp.tanh(math.sqrt(2.0 / math.pi) * (x + 0.044715 * (x * x * x))))


def _ffn_kernel(alpha, ff_chunk, x_ref, xp_ref, xn_ref, wg_ref, wu_ref, cw_ref, cb_ref, wd_ref,
                lg_ref, lb_ref, o_ref, h_sc):
    s = pl.program_id(1)
    ns = pl.num_programs(1)
    tm = x_ref.shape[1]
    d_ff = wg_ref.shape[1]
    halo = xp_ref.shape[1]
    x = x_ref[0]
    xb = x.astype(BF16)
    xh = jnp.concatenate([xp_ref[0], xn_ref[0]], axis=0).astype(BF16)
    has_prev = (s > 0).astype(F32)
    has_next = (s < ns - 1).astype(F32)
    row = lax.broadcasted_iota(jnp.int32, (tm, 1), 0)
    for c in range(d_ff // ff_chunk):
        cs = slice(c * ff_chunk, (c + 1) * ff_chunk)
        wg = wg_ref[:, cs]
        g = jnp.dot(xb, wg, preferred_element_type=F32)
        u = jnp.dot(xb, wu_ref[:, cs], preferred_element_type=F32)
        gh = jnp.dot(xh, wg, preferred_element_type=F32)
        g_before = gh[halo - 1:halo] * has_prev
        g_after = gh[halo:halo + 1] * has_next
        g_m1 = jnp.where(row == 0, g_before, pltpu.roll(g, 1, 0))
        g_p1 = jnp.where(row == tm - 1, g_after, pltpu.roll(g, tm - 1, 0))
        gc = cb_ref[:, cs] + g_m1 * cw_ref[0:1, cs] + g * cw_ref[1:2, cs] + g_p1 * cw_ref[2:3, cs]
        h_sc[:, cs] = (_gelu_tanh(gc) * u).astype(BF16)
    ffn = jnp.dot(h_sc[...], wd_ref[...], preferred_element_type=F32)
    o_ref[0] = _layer_norm(alpha * x + ffn, lg_ref[...], lb_ref[...])


def _ffn(x, wg, wu, cw, cb, wd, lg, lb, *, alpha, tm, ff_chunk, halo=8):
    B, S, D = x.shape
    d_ff = wg.shape[1]
    ns = S // tm
    hb = tm // halo
    last = S // halo - 1
    return pl.pallas_call(
        functools.partial(_ffn_kernel, alpha, ff_chunk),
        grid=(B, ns),
        in_specs=[
            pl.BlockSpec((1, tm, D), lambda b, s: (b, s, 0)),
            pl.BlockSpec((1, halo, D), lambda b, s: (b, jnp.maximum(s * hb - 1, 0), 0)),
            pl.BlockSpec((1, halo, D), lambda b, s: (b, jnp.minimum((s + 1) * hb, last), 0)),
            _resident((D, d_ff)), _resident((D, d_ff)), _resident((CONV_W, d_ff)), _resident((1, d_ff)),
            _resident((d_ff, D)), _resident((1, D)), _resident((1, D)),
        ],
        out_specs=pl.BlockSpec((1, tm, D), lambda b, s: (b, s, 0)),
        out_shape=jax.ShapeDtypeStruct((B, S, D), F32),
        scratch_shapes=[pltpu.VMEM((tm, d_ff), BF16)],
        compiler_params=_cparams(("parallel", "parallel")),
        name="ffn",
    )(x, x, x, wg, wu, cw, cb, wd, lg, lb)


def _rope_tables(seq_len):
    rows_n = seq_len // GRID_W
    row = jnp.repeat(jnp.arange(rows_n, dtype=F32), GRID_W)
    col = jnp.tile(jnp.arange(GRID_W, dtype=F32), rows_n)
    half = HEAD_DIM // 2
    inv_freq = ROPE_THETA ** (-jnp.arange(0, half, 2, dtype=F32) / half)
    ang = jnp.concatenate([row[:, None] * inv_freq, col[:, None] * inv_freq], axis=-1)
    c, s = jnp.cos(ang), jnp.sin(ang)
    return jnp.concatenate([c, c, c, c], axis=-1), jnp.concatenate([-s, s, -s, s], axis=-1)


def _t5_bucket(rel):
    half = N_BUCKETS // 2
    max_exact = half // 2
    bucket = jnp.where(rel > 0, half, 0)
    rp = jnp.abs(rel)
    rpf = jnp.maximum(rp, 1).astype(F32)
    large = max_exact + (jnp.log(rpf / max_exact) / math.log(MAX_DISTANCE / max_exact)
                         * (half - max_exact)).astype(jnp.int32)
    large = jnp.minimum(large, half - 1)
    return bucket + jnp.where(rp < max_exact, rp, large)


def _deinterleave_perm(n_heads):
    per_head = np.concatenate([np.arange(0, HEAD_DIM, 2), np.arange(1, HEAD_DIM, 2)])
    return np.concatenate([h * HEAD_DIM + per_head for h in range(n_heads)])


def kernel(x, rel_bias, w_in, q_norm, k_norm, sink, out_norm_a, out_norm_b, w_out,
           ln1_g, ln1_b, w_gate, w_up, conv_w, conv_b, w_down, ln2_g, ln2_b):
    B, S, D = x.shape
    depth = w_in.shape[0]
    alpha = (2.0 * depth) ** 0.25

    cos_t, sin_t = _rope_tables(S)
    qpos = jnp.arange(W_TQ, dtype=jnp.int32)
    kpos = jnp.arange(W_TK, dtype=jnp.int32) - WINDOW
    bucket = _t5_bucket(kpos[:, None] - qpos[None, :]).astype(jnp.int32)
    blk = np.arange(LANES) // HEAD_DIM
    ones_blk = jnp.asarray(blk[:, None] == blk[None, :], dtype=BF16)

    perm_q = _deinterleave_perm(HEADS_A)
    perm_k = _deinterleave_perm(KV_A)
    col_perm = np.concatenate([perm_q, Q_A + perm_k, np.arange(Q_A + KVD_A, w_in.shape[2])])
    head_perm = _deinterleave_perm(1)

    for l in range(depth):
        w = w_in[l][:, col_perm].astype(BF16)
        gq_max = jnp.max(jnp.abs(q_norm[l])) * (SM_SCALE * LOG2E)
        gk_max = jnp.max(jnp.abs(k_norm[l]))
        bal = jnp.exp2(jnp.round(0.5 * jnp.log2(jnp.maximum(gq_max, 1e-30) / jnp.maximum(gk_max, 1e-30))))
        gq = jnp.tile(q_norm[l][head_perm] * (SM_SCALE * LOG2E) / bal, HEADS_A)[None, :]
        gk = jnp.tile(k_norm[l][head_perm] * bal, KV_A)[None, :]

        def front(use_f8, x, w, gq, gk):
            qat, ka, vat, qbt, kb, vbt = _inproj(x, w, gq, gk, cos_t, sin_t, ones_blk,
                                                 tm=512, tkc=1024, tq=256, use_f8=use_f8)
            return _gattn(qat, ka, vat), qbt, kb, vbt

        bound = math.sqrt(HEAD_DIM) * jnp.maximum(gk_max * bal, gq_max / bal)
        ya, qbt, kb, vbt = lax.cond(bound <= F8_SAFE_MAX, functools.partial(front, True),
                                    functools.partial(front, False), x, w, gq, gk)
        yb = _wattn(rel_bias, sink[l], bucket, qbt, kb, vbt)
        x = _outproj(ya, yb, x, out_norm_a[l][None, :], out_norm_b[l][None, :],
                     w_out[l][:Q_A].astype(BF16), w_out[l][Q_A:].astype(BF16),
                     ln1_g[l][None, :], ln1_b[l][None, :], alpha=alpha, tm=512)
        x = _ffn(x, w_gate[l].astype(BF16), w_up[l].astype(BF16), conv_w[l], conv_b[l][None, :],
                 w_down[l].astype(BF16), ln2_g[l][None, :], ln2_b[l][None, :],
                 alpha=alpha, tm=512, ff_chunk=256)
    return x
```

```python
import functools
import math

import jax
import jax.numpy as jnp
import numpy as np
from jax import lax
from jax.experimental import pallas as pl
from jax.experimental.pallas import tpu as pltpu

F32 = jnp.float32
BF16 = jnp.bfloat16
F8 = jnp.float8_e4m3fn
F8_SLOTS = 3
F8_SAFE_MAX = 384.0

HEAD_DIM = 64
HEADS_A = 8
HEADS_B = 8
KV_A = 2
KV_B = 2
G_A = HEADS_A // KV_A
G_B = HEADS_B // KV_B
Q_A = HEADS_A * HEAD_DIM
KVD_A = KV_A * HEAD_DIM
Q_B = HEADS_B * HEAD_DIM
KVD_B = KV_B * HEAD_DIM
GRID_W = 64
ROPE_THETA = 10000.0
Q_BLOCK = 128
WINDOW = 128
N_BUCKETS = 32
MAX_DISTANCE = 128
CONV_W = 3
RMS_EPS = 1e-6
LN_EPS = 1e-5
LOG2E = math.log2(math.e)
SM_SCALE = HEAD_DIM ** -0.5

LANES = 128
V_ROWS = 80
MXU_TILE = 256
ROW_TILE = 1024
FF_CHUNK = MXU_TILE
G_TQ = MXU_TILE
G_CHUNK = 1024
G_PIECE = MXU_TILE
G_SCORE_PIECES = G_CHUNK // G_PIECE
VMEM_LIMIT = 56 * 1024 * 1024


def _cparams(sem):
    return pltpu.CompilerParams(dimension_semantics=sem, vmem_limit_bytes=VMEM_LIMIT)


def _resident(shape):
    nd = len(shape)
    return pl.BlockSpec(shape, lambda *_: (0,) * nd, pipeline_mode=pl.Buffered(1))


def _split_f8(x):
    hi = x.astype(F8).astype(F32)
    lo = ((x - hi) * 4.0).astype(F8).astype(F32)
    hi4 = (hi * 0.25).astype(F8).astype(F32)
    return hi, lo, hi4


def _inproj_kernel(use_f8, x_ref, w_ref, gq_ref, gk_ref, cos_ref, sin_ref, j_ref,
                   qat_ref, ka_ref, vat_ref, qbt_ref, kb_ref, vbt_ref):
    tm = x_ref.shape[1]
    tq = qat_ref.shape[4]
    x = x_ref[0].astype(BF16)

    lane = lax.broadcasted_iota(jnp.int32, (tm, LANES), 1)
    first_half = (lane % HEAD_DIM) < (HEAD_DIM // 2)
    row = lax.broadcasted_iota(jnp.int32, (V_ROWS - HEAD_DIM, tm), 0)
    ones_rows = jnp.where(row == 0, 1.0, 0.0).astype(BF16)

    def project(c0, c1):
        return jnp.dot(x, w_ref[:, c0:c1], preferred_element_type=F32)

    def norm_rope(hc, gain):
        ssq = jnp.dot((hc * hc).astype(BF16), j_ref[...], preferred_element_type=F32)
        y = hc * lax.rsqrt(ssq * (1.0 / HEAD_DIM) + RMS_EPS) * gain
        partner = jnp.where(first_half, pltpu.roll(y, LANES - HEAD_DIM // 2, 1),
                            pltpu.roll(y, HEAD_DIM // 2, 1))
        return y * cos_ref[...] + partner * sin_ref[...]

    def finish_qa(h):
        q = jnp.concatenate(
            [norm_rope(h[:, c * LANES:(c + 1) * LANES], gq_ref[:, c * LANES:(c + 1) * LANES])
             for c in range(Q_A // LANES)], axis=1)
        qt = q.T
        if use_f8:
            qt_hi, qt_lo, qt_hi4 = _split_f8(qt)
            parts = (qt_hi, qt_hi4, qt_lo)
        else:
            parts = (qt,)
        for hh in range(HEADS_A):
            rows = slice(hh * HEAD_DIM, (hh + 1) * HEAD_DIM)
            for c in range(tm // tq):
                cols = slice(c * tq, (c + 1) * tq)
                for slot, part in enumerate(parts):
                    qat_ref[0, hh, c, slot * HEAD_DIM:(slot + 1) * HEAD_DIM, :] = (
                        part[rows, cols].astype(qat_ref.dtype))

    def finish_kva(h):
        k = norm_rope(h[:, 0:KVD_A], gk_ref[...])
        if use_f8:
            k_hi, k_lo, k_hi4 = _split_f8(k)
        vt = h[:, KVD_A:2 * KVD_A].T
        for g in range(KV_A):
            gl = slice(g * HEAD_DIM, (g + 1) * HEAD_DIM)
            if use_f8:
                ka_ref[0, g] = jnp.concatenate([k_hi[:, gl], k_lo[:, gl], k_hi4[:, gl]], axis=1).astype(F8)
            else:
                ka_ref[0, g] = k[:, gl].astype(BF16)
            vat_ref[0, g, 0, 0:HEAD_DIM, :] = vt[gl].astype(BF16)
            vat_ref[0, g, 0, HEAD_DIM:V_ROWS, :] = ones_rows

    def finish_qb(h):
        qbt = (h * (SM_SCALE * LOG2E)).T
        for hh in range(HEADS_B):
            qbt_ref[0, hh] = qbt[hh * HEAD_DIM:(hh + 1) * HEAD_DIM].astype(BF16)

    def finish_kvb(h):
        kb = h[:, 0:KVD_B]
        vbt = h[:, KVD_B:2 * KVD_B].T
        for g in range(KV_B):
            gl = slice(g * HEAD_DIM, (g + 1) * HEAD_DIM)
            kb_ref[0, g] = kb[:, gl].astype(BF16)
            vbt_ref[0, g, 0:HEAD_DIM, :] = vbt[gl].astype(BF16)
            vbt_ref[0, g, HEAD_DIM:V_ROWS, :] = ones_rows

    c_kva = Q_A
    c_qb = Q_A + 2 * KVD_A
    c_kvb = c_qb + Q_B
    h_qa = project(0, c_kva)
    h_kva = project(c_kva, c_qb)
    finish_qa(h_qa)
    h_qb = project(c_qb, c_kvb)
    finish_kva(h_kva)
    h_kvb = project(c_kvb, c_kvb + 2 * KVD_B)
    finish_qb(h_qb)
    finish_kvb(h_kvb)


def _inproj(x, w, gq, gk, cos_t, sin_t, ones_blk, *, tm, tkc, tq, use_f8):
    B, S, D = x.shape
    ns = S // tm
    n_cols = w.shape[1]
    per_chunk = tkc // tm
    qk_dim = (F8_SLOTS if use_f8 else 1) * HEAD_DIM
    qk_dtype = F8 if use_f8 else BF16
    out_shape = (
        jax.ShapeDtypeStruct((B, HEADS_A, S // tq, qk_dim, tq), qk_dtype),
        jax.ShapeDtypeStruct((B, KV_A, S, qk_dim), qk_dtype),
        jax.ShapeDtypeStruct((B, KV_A, S // tkc, V_ROWS, tkc), BF16),
        jax.ShapeDtypeStruct((B, HEADS_B, HEAD_DIM, S), BF16),
        jax.ShapeDtypeStruct((B, KV_B, S, HEAD_DIM), BF16),
        jax.ShapeDtypeStruct((B, KV_B, V_ROWS, S), BF16),
    )
    return pl.pallas_call(
        functools.partial(_inproj_kernel, use_f8),
        grid=(B, ns),
        in_specs=[
            pl.BlockSpec((1, tm, D), lambda b, s: (b, s, 0)),
            _resident((D, n_cols)),
            _resident((1, Q_A)),
            _resident((1, KVD_A)),
            pl.BlockSpec((tm, LANES), lambda b, s: (s, 0)),
            pl.BlockSpec((tm, LANES), lambda b, s: (s, 0)),
            _resident((LANES, LANES)),
        ],
        out_specs=(
            pl.BlockSpec((1, HEADS_A, tm // tq, qk_dim, tq), lambda b, s: (b, 0, s, 0, 0)),
            pl.BlockSpec((1, KV_A, tm, qk_dim), lambda b, s: (b, 0, s, 0)),
            pl.BlockSpec((1, KV_A, 1, V_ROWS, tm), lambda b, s: (b, 0, s // per_chunk, 0, s % per_chunk)),
            pl.BlockSpec((1, HEADS_B, HEAD_DIM, tm), lambda b, s: (b, 0, 0, s)),
            pl.BlockSpec((1, KV_B, tm, HEAD_DIM), lambda b, s: (b, 0, s, 0)),
            pl.BlockSpec((1, KV_B, V_ROWS, tm), lambda b, s: (b, 0, 0, s)),
        ),
        out_shape=out_shape,
        compiler_params=_cparams(("parallel", "parallel")),
        name="inproj",
    )(x, w, gq, gk, cos_t, sin_t, ones_blk)


def _gattn_kernel(qt_ref, k_ref, vt_ref, o_ref, s_sc, mnew_sc, alpha_sc, m_sc, acc_sc):
    n_chunks, _, tkc = vt_ref.shape[2:]
    nq, _, tq = qt_ref.shape[2:]
    assert n_chunks % 2 == 0 and n_chunks >= 4
    n_pieces = tkc // G_PIECE

    def reset_max():
        m_sc[...] = jnp.full(m_sc.shape, -jnp.inf, F32)

    def score_piece(qi, j, buf, h, pc):
        rows = pl.ds(pl.multiple_of(j * tkc, tkc) + pc * G_PIECE, G_SCORE_PIECES * G_PIECE)
        st = jnp.dot(k_ref[0, 0, rows, :], qt_ref[0, h, qi], preferred_element_type=F32)
        s_sc[buf, h, pc * G_PIECE:(pc + G_SCORE_PIECES) * G_PIECE, :] = st
        return jnp.max(st, axis=0, keepdims=True)

    def value_piece(j, buf, h, pc):
        cols = slice(pc * G_PIECE, (pc + 1) * G_PIECE)
        pt = jnp.exp2(s_sc[buf, h, cols, :] - mnew_sc[buf, h]).astype(BF16)
        return jnp.dot(vt_ref[0, 0, j, :, cols], pt, preferred_element_type=F32)

    def stage(score_of, value_of):
        for h in range(G_A):
            piece_max, pv = [], None
            for pc in range(n_pieces):
                if score_of is not None and pc % G_SCORE_PIECES == 0:
                    piece_max.append(score_piece(*score_of, h, pc))
                if value_of is not None:
                    d = value_piece(*value_of, h, pc)
                    pv = d if pv is None else pv + d
            if score_of is not None:
                buf = score_of[2]
                m_prev = m_sc[h]
                m_new = functools.reduce(jnp.maximum, piece_max, m_prev)
                alpha_sc[buf, h] = jnp.exp2(m_prev - m_new)
                mnew_sc[buf, h] = m_new
                m_sc[h] = m_new
            if value_of is not None:
                acc_sc[h] = alpha_sc[value_of[1], h] * acc_sc[h] + pv

    def middle_chunks(qi):
        def pair(i, carry):
            stage((qi, 2 * i + 1, 1), (2 * i, 0))
            stage((qi, 2 * i + 2, 0), (2 * i + 1, 1))
            return carry

        lax.fori_loop(0, n_chunks // 2 - 1, pair, 0, unroll=True)
        stage((qi, n_chunks - 1, 1), (n_chunks - 2, 0))

    def finalize(qi):
        outs = []
        for h in range(G_A):
            a = acc_sc[h]
            outs.append(a[0:HEAD_DIM] / a[HEAD_DIM:HEAD_DIM + 1])
        rows = pl.ds(pl.multiple_of(qi * tq, tq), tq)
        o_ref[0, rows, :] = jnp.concatenate(outs, axis=0).T.astype(o_ref.dtype)

    acc_sc[...] = jnp.zeros(acc_sc.shape, F32)
    reset_max()
    stage((0, 0, 0), None)

    def q_block(qi, carry):
        middle_chunks(qi)
        reset_max()
        stage((qi + 1, 0, 0), (n_chunks - 1, 1))
        finalize(qi)
        return carry

    lax.fori_loop(0, nq - 1, q_block, 0)
    middle_chunks(nq - 1)
    stage(None, (n_chunks - 1, 1))
    finalize(nq - 1)


def _gattn(qat, ka, vat):
    B, _, nq, qk_dim, tq = qat.shape
    S = nq * tq
    n_chunks, _, tkc = vat.shape[2:]
    return pl.pallas_call(
        _gattn_kernel,
        grid=(B, KV_A),
        in_specs=[
            pl.BlockSpec((1, G_A, nq, qk_dim, tq), lambda b, g: (b, g, 0, 0, 0)),
            pl.BlockSpec((1, 1, S, qk_dim), lambda b, g: (b, g, 0, 0)),
            pl.BlockSpec((1, 1, n_chunks, V_ROWS, tkc), lambda b, g: (b, g, 0, 0, 0)),
        ],
        out_specs=pl.BlockSpec((1, S, G_A * HEAD_DIM), lambda b, g: (b, 0, g)),
        out_shape=jax.ShapeDtypeStruct((B, S, Q_A), BF16),
        scratch_shapes=[
            pltpu.VMEM((2, G_A, tkc, tq), F32),
            pltpu.VMEM((2, G_A, 1, tq), F32),
            pltpu.VMEM((2, G_A, 1, tq), F32),
            pltpu.VMEM((G_A, 1, tq), F32),
            pltpu.VMEM((G_A, V_ROWS, tq), F32),
        ],
        compiler_params=_cparams(("parallel", "parallel")),
        name="gattn",
    )(qat, ka, vat)


W_TQ = 2 * Q_BLOCK
W_TK = W_TQ + 2 * WINDOW
W_NBLK = W_TK // Q_BLOCK


def _wattn_kernel(rb_ref, sink_ref, bucket_ref, qt_ref, *refs):
    k_refs = refs[:W_NBLK]
    v_refs = refs[W_NBLK:2 * W_NBLK]
    o_ref, bias_sc, s_sc = refs[2 * W_NBLK:]
    b = pl.program_id(0)
    i = pl.program_id(1)
    n = pl.num_programs(1)

    @pl.when((b == 0) & (i == 0))
    def _():
        bucket = bucket_ref[...]
        krow = lax.broadcasted_iota(jnp.int32, (W_TK, W_TQ), 0)
        qcol = lax.broadcasted_iota(jnp.int32, (W_TK, W_TQ), 1)
        in_window = jnp.abs(krow - WINDOW - qcol) <= WINDOW
        for h in range(HEADS_B):
            t = jnp.zeros((W_TK, W_TQ), F32)
            for bk in range(N_BUCKETS):
                t = jnp.where(bucket == bk, rb_ref[bk, h], t)
            t = jnp.where(in_window, t * LOG2E, -jnp.inf)
            bias_sc[0, h] = jnp.where(krow >= Q_BLOCK, t, -jnp.inf)
            bias_sc[1, h] = t
            bias_sc[2, h] = jnp.where(krow < W_TK - Q_BLOCK, t, -jnp.inf)

    table = jnp.where(i == 0, 0, jnp.where(i == n - 1, 2, 1))
    kws = [jnp.concatenate([r[0, g] for r in k_refs], axis=0) for g in range(KV_B)]
    vws = [jnp.concatenate([r[0, g] for r in v_refs], axis=1) for g in range(KV_B)]
    sinks = [sink_ref[h] * LOG2E for h in range(HEADS_B)]
    maxes = [None] * HEADS_B

    def scores(h):
        st = jnp.dot(kws[h // G_B], qt_ref[0, h], preferred_element_type=F32) + bias_sc[table, h]
        s_sc[h] = st
        maxes[h] = jnp.maximum(jnp.max(st, axis=0, keepdims=True), sinks[h])

    def finish(h):
        pt = jnp.exp2(s_sc[h] - maxes[h]).astype(BF16)
        acc = jnp.dot(vws[h // G_B], pt, preferred_element_type=F32)
        denom = acc[HEAD_DIM:HEAD_DIM + 1] + jnp.exp2(sinks[h] - maxes[h])
        return acc[0:HEAD_DIM] / denom

    ahead = 4
    for h in range(ahead):
        scores(h)
    outs = []
    for h in range(HEADS_B):
        if h + ahead < HEADS_B:
            scores(h + ahead)
        outs.append(finish(h))
    o_ref[0] = jnp.concatenate(outs, axis=0).T.astype(o_ref.dtype)


def _wattn(rel_bias, sink, bucket, qbt, kb, vbt):
    B, _, _, S = qbt.shape
    n = S // W_TQ
    last_blk = S // Q_BLOCK - 1
    smem = pl.BlockSpec(memory_space=pltpu.SMEM)

    def blk(c):
        return lambda b, i: jnp.clip(i * (W_TQ // Q_BLOCK) - 1 + c, 0, last_blk)

    k_specs = [pl.BlockSpec((1, KV_B, Q_BLOCK, HEAD_DIM), lambda b, i, f=blk(c): (b, 0, f(b, i), 0))
               for c in range(W_NBLK)]
    v_specs = [pl.BlockSpec((1, KV_B, V_ROWS, Q_BLOCK), lambda b, i, f=blk(c): (b, 0, 0, f(b, i)))
               for c in range(W_NBLK)]
    return pl.pallas_call(
        _wattn_kernel,
        grid=(B, n),
        in_specs=[smem, smem, _resident((W_TK, W_TQ)),
                  pl.BlockSpec((1, HEADS_B, HEAD_DIM, W_TQ), lambda b, i: (b, 0, 0, i))] + k_specs + v_specs,
        out_specs=pl.BlockSpec((1, W_TQ, Q_B), lambda b, i: (b, i, 0)),
        out_shape=jax.ShapeDtypeStruct((B, S, Q_B), BF16),
        scratch_shapes=[pltpu.VMEM((3, HEADS_B, W_TK, W_TQ), F32), pltpu.VMEM((HEADS_B, W_TK, W_TQ), F32)],
        compiler_params=_cparams(("arbitrary", "arbitrary")),
        name="wattn",
    )(rel_bias, sink, bucket, qbt, *([kb] * W_NBLK), *([vbt] * W_NBLK))


def _layer_norm(y, g, b):
    mu = jnp.mean(y, axis=-1, keepdims=True)
    d = y - mu
    var = jnp.mean(d * d, axis=-1, keepdims=True)
    return d * lax.rsqrt(var + LN_EPS) * g + b


def _rms(y, g):
    yf = y.astype(F32)
    return yf * lax.rsqrt(jnp.mean(yf * yf, axis=-1, keepdims=True) + RMS_EPS) * g


def _outproj_kernel(alpha, ya_ref, yb_ref, x_ref, ga_ref, gb_ref, wa_ref, wb_ref, lg_ref, lb_ref, o_ref):
    na = _rms(ya_ref[0], ga_ref[...]).astype(BF16)
    nb = _rms(yb_ref[0], gb_ref[...]).astype(BF16)
    mix = (jnp.dot(na, wa_ref[...], preferred_element_type=F32)
           + jnp.dot(nb, wb_ref[...], preferred_element_type=F32))
    o_ref[0] = _layer_norm(alpha * x_ref[0] + mix, lg_ref[...], lb_ref[...])


def _outproj(ya, yb, x, ga, gb, wa, wb, lg, lb, *, alpha, tm):
    B, S, D = x.shape
    row = lambda w: pl.BlockSpec((1, tm, w), lambda b, s: (b, s, 0))
    return pl.pallas_call(
        functools.partial(_outproj_kernel, alpha),
        grid=(B, S // tm),
        in_specs=[row(Q_A), row(Q_B), row(D), _resident((1, Q_A)), _resident((1, Q_B)),
                  _resident((Q_A, D)), _resident((Q_B, D)), _resident((1, D)), _resident((1, D))],
        out_specs=row(D),
        out_shape=jax.ShapeDtypeStruct((B, S, D), F32),
        compiler_params=_cparams(("parallel", "parallel")),
        name="outproj",
    )(ya, yb, x, ga, gb, wa, wb, lg, lb)


def _gelu_tanh(x):
    return 0.5 * x * (1.0 + jnp.tanh(math.sqrt(2.0 / math.pi) * (x + 0.044715 * (x * x * x))))


def _ffn_kernel(alpha, ff_chunk, x_ref, xp_ref, xn_ref, wg_ref, wu_ref, cw_ref, cb_ref, wd_ref,
                lg_ref, lb_ref, o_ref, h_sc):
    s = pl.program_id(1)
    ns = pl.num_programs(1)
    tm = x_ref.shape[1]
    d_ff = wg_ref.shape[1]
    halo = xp_ref.shape[1]
    x = x_ref[0]
    xb = x.astype(BF16)
    xh = jnp.concatenate([xp_ref[0], xn_ref[0]], axis=0).astype(BF16)
    has_prev = (s > 0).astype(F32)
    has_next = (s < ns - 1).astype(F32)
    row = lax.broadcasted_iota(jnp.int32, (tm, 1), 0)
    for c in range(d_ff // ff_chunk):
        cs = slice(c * ff_chunk, (c + 1) * ff_chunk)
        wg = wg_ref[:, cs]
        g = jnp.dot(xb, wg, preferred_element_type=F32)
        u = jnp.dot(xb, wu_ref[:, cs], preferred_element_type=F32)
        gh = jnp.dot(xh, wg, preferred_element_type=F32)
        g_before = gh[halo - 1:halo] * has_prev
        g_after = gh[halo:halo + 1] * has_next
        g_m1 = jnp.where(row == 0, g_before, pltpu.roll(g, 1, 0))
        g_p1 = jnp.where(row == tm - 1, g_after, pltpu.roll(g, tm - 1, 0))
        gc = cb_ref[:, cs] + g_m1 * cw_ref[0:1, cs] + g * cw_ref[1:2, cs] + g_p1 * cw_ref[2:3, cs]
        h_sc[:, cs] = (_gelu_tanh(gc) * u).astype(BF16)
    ffn = jnp.dot(h_sc[...], wd_ref[...], preferred_element_type=F32)
    o_ref[0] = _layer_norm(alpha * x + ffn, lg_ref[...], lb_ref[...])


def _ffn(x, wg, wu, cw, cb, wd, lg, lb, *, alpha, tm, ff_chunk, halo=8):
    B, S, D = x.shape
    d_ff = wg.shape[1]
    ns = S // tm
    hb = tm // halo
    last = S // halo - 1
    return pl.pallas_call(
        functools.partial(_ffn_kernel, alpha, ff_chunk),
        grid=(B, ns),
        in_specs=[
            pl.BlockSpec((1, tm, D), lambda b, s: (b, s, 0)),
            pl.BlockSpec((1, halo, D), lambda b, s: (b, jnp.maximum(s * hb - 1, 0), 0)),
            pl.BlockSpec((1, halo, D), lambda b, s: (b, jnp.minimum((s + 1) * hb, last), 0)),
            _resident((D, d_ff)), _resident((D, d_ff)), _resident((CONV_W, d_ff)), _resident((1, d_ff)),
            _resident((d_ff, D)), _resident((1, D)), _resident((1, D)),
        ],
        out_specs=pl.BlockSpec((1, tm, D), lambda b, s: (b, s, 0)),
        out_shape=jax.ShapeDtypeStruct((B, S, D), F32),
        scratch_shapes=[pltpu.VMEM((tm, d_ff), BF16)],
        compiler_params=_cparams(("parallel", "parallel")),
        name="ffn",
    )(x, x, x, wg, wu, cw, cb, wd, lg, lb)


def _rope_tables(seq_len):
    rows_n = seq_len // GRID_W
    row = jnp.repeat(jnp.arange(rows_n, dtype=F32), GRID_W)
    col = jnp.tile(jnp.arange(GRID_W, dtype=F32), rows_n)
    half = HEAD_DIM // 2
    inv_freq = ROPE_THETA ** (-jnp.arange(0, half, 2, dtype=F32) / half)
    ang = jnp.concatenate([row[:, None] * inv_freq, col[:, None] * inv_freq], axis=-1)
    c, s = jnp.cos(ang), jnp.sin(ang)
    return jnp.concatenate([c, c, c, c], axis=-1), jnp.concatenate([-s, s, -s, s], axis=-1)


def _t5_bucket(rel):
    half = N_BUCKETS // 2
    max_exact = half // 2
    bucket = jnp.where(rel > 0, half, 0)
    rp = jnp.abs(rel)
    rpf = jnp.maximum(rp, 1).astype(F32)
    large = max_exact + (jnp.log(rpf / max_exact) / math.log(MAX_DISTANCE / max_exact)
                         * (half - max_exact)).astype(jnp.int32)
    large = jnp.minimum(large, half - 1)
    return bucket + jnp.where(rp < max_exact, rp, large)


def _deinterleave_perm(n_heads):
    per_head = np.concatenate([np.arange(0, HEAD_DIM, 2), np.arange(1, HEAD_DIM, 2)])
    return np.concatenate([h * HEAD_DIM + per_head for h in range(n_heads)])


def kernel(x, rel_bias, w_in, q_norm, k_norm, sink, out_norm_a, out_norm_b, w_out,
           ln1_g, ln1_b, w_gate, w_up, conv_w, conv_b, w_down, ln2_g, ln2_b):
    B, S, D = x.shape
    depth = w_in.shape[0]
    alpha = (2.0 * depth) ** 0.25

    cos_t, sin_t = _rope_tables(S)
    qpos = jnp.arange(W_TQ, dtype=jnp.int32)
    kpos = jnp.arange(W_TK, dtype=jnp.int32) - WINDOW
    bucket = _t5_bucket(kpos[:, None] - qpos[None, :]).astype(jnp.int32)
    blk = np.arange(LANES) // HEAD_DIM
    ones_blk = jnp.asarray(blk[:, None] == blk[None, :], dtype=BF16)

    perm_q = _deinterleave_perm(HEADS_A)
    perm_k = _deinterleave_perm(KV_A)
    col_perm = np.concatenate([perm_q, Q_A + perm_k, np.arange(Q_A + KVD_A, w_in.shape[2])])
    head_perm = _deinterleave_perm(1)

    w_in_b = w_in[:, :, col_perm].astype(BF16)
    w_out_b = w_out.astype(BF16)
    w_gate_b, w_up_b, w_down_b = w_gate.astype(BF16), w_up.astype(BF16), w_down.astype(BF16)

    for l in range(depth):
        w = w_in_b[l]
        gq_max = jnp.max(jnp.abs(q_norm[l])) * (SM_SCALE * LOG2E)
        gk_max = jnp.max(jnp.abs(k_norm[l]))
        bal = jnp.exp2(jnp.round(0.5 * jnp.log2(jnp.maximum(gq_max, 1e-30) / jnp.maximum(gk_max, 1e-30))))
        gq = jnp.tile(q_norm[l][head_perm] * (SM_SCALE * LOG2E) / bal, HEADS_A)[None, :]
        gk = jnp.tile(k_norm[l][head_perm] * bal, KV_A)[None, :]

        def front(use_f8, x, w, gq, gk):
            qat, ka, vat, qbt, kb, vbt = _inproj(x, w, gq, gk, cos_t, sin_t, ones_blk,
                                                 tm=ROW_TILE, tkc=G_CHUNK, tq=G_TQ, use_f8=use_f8)
            return _gattn(qat, ka, vat), qbt, kb, vbt

        bound = math.sqrt(HEAD_DIM) * jnp.maximum(gk_max * bal, gq_max / bal)
        ya, qbt, kb, vbt = lax.cond(bound <= F8_SAFE_MAX, functools.partial(front, True),
                                    functools.partial(front, False), x, w, gq, gk)
        yb = _wattn(rel_bias, sink[l], bucket, qbt, kb, vbt)
        x = _outproj(ya, yb, x, out_norm_a[l][None, :], out_norm_b[l][None, :],
                     w_out_b[l, :Q_A], w_out_b[l, Q_A:],
                     ln1_g[l][None, :], ln1_b[l][None, :], alpha=alpha, tm=ROW_TILE)
        x = _ffn(x, w_gate_b[l], w_up_b[l], conv_w[l], conv_b[l][None, :],
                 w_down_b[l], ln2_g[l][None, :], ln2_b[l][None, :],
                 alpha=alpha, tm=ROW_TILE, ff_chunk=FF_CHUNK)
    return x
```

```python
import functools
import math

import jax
import jax.numpy as jnp
import numpy as np
from jax import lax
from jax.experimental import pallas as pl
from jax.experimental.pallas import tpu as pltpu

F32 = jnp.float32
BF16 = jnp.bfloat16
F8 = jnp.float8_e4m3fn
F8_SLOTS = 3
F8_SAFE_MAX = 384.0

HEAD_DIM = 64
HEADS_A = 8
HEADS_B = 8
KV_A = 2
KV_B = 2
G_A = HEADS_A // KV_A
G_B = HEADS_B // KV_B
Q_A = HEADS_A * HEAD_DIM
KVD_A = KV_A * HEAD_DIM
Q_B = HEADS_B * HEAD_DIM
KVD_B = KV_B * HEAD_DIM
GRID_W = 64
ROPE_THETA = 10000.0
Q_BLOCK = 128
WINDOW = 128
N_BUCKETS = 32
MAX_DISTANCE = 128
CONV_W = 3
RMS_EPS = 1e-6
LN_EPS = 1e-5
LOG2E = math.log2(math.e)
SM_SCALE = HEAD_DIM ** -0.5

LANES = 128
V_ROWS = 80
MXU_TILE = 256
ROW_TILE = 1024
FF_CHUNK = MXU_TILE
G_TQ = MXU_TILE
G_CHUNK = 1024
G_PIECE = MXU_TILE
G_SCORE_PIECES = G_CHUNK // G_PIECE
VMEM_LIMIT = 56 * 1024 * 1024


def _cparams(sem):
    return pltpu.CompilerParams(dimension_semantics=sem, vmem_limit_bytes=VMEM_LIMIT)


def _resident(shape):
    nd = len(shape)
    return pl.BlockSpec(shape, lambda *_: (0,) * nd, pipeline_mode=pl.Buffered(1))


def _split_f8(x):
    hi = x.astype(F8).astype(F32)
    lo = ((x - hi) * 4.0).astype(F8).astype(F32)
    hi4 = (hi * 0.25).astype(F8).astype(F32)
    return hi, lo, hi4


def _inproj_kernel(use_f8, x_ref, w_ref, gq_ref, gk_ref, cos_ref, sin_ref, j_ref,
                   qat_ref, ka_ref, vat_ref, qbt_ref, kb_ref, vbt_ref):
    tm = x_ref.shape[1]
    tq = qat_ref.shape[4]
    x = x_ref[0].astype(BF16)

    lane = lax.broadcasted_iota(jnp.int32, (tm, LANES), 1)
    first_half = (lane % HEAD_DIM) < (HEAD_DIM // 2)
    row = lax.broadcasted_iota(jnp.int32, (V_ROWS - HEAD_DIM, tm), 0)
    ones_rows = jnp.where(row == 0, 1.0, 0.0).astype(BF16)

    def project(c0, c1):
        return jnp.dot(x, w_ref[:, c0:c1], preferred_element_type=F32)

    def norm_rope(hc, gain):
        ssq = jnp.dot((hc * hc).astype(BF16), j_ref[...], preferred_element_type=F32)
        y = hc * lax.rsqrt(ssq * (1.0 / HEAD_DIM) + RMS_EPS) * gain
        partner = jnp.where(first_half, pltpu.roll(y, LANES - HEAD_DIM // 2, 1),
                            pltpu.roll(y, HEAD_DIM // 2, 1))
        return y * cos_ref[...] + partner * sin_ref[...]

    def finish_qa(h):
        q = jnp.concatenate(
            [norm_rope(h[:, c * LANES:(c + 1) * LANES], gq_ref[:, c * LANES:(c + 1) * LANES])
             for c in range(Q_A // LANES)], axis=1)
        qt = q.T
        if use_f8:
            qt_hi, qt_lo, qt_hi4 = _split_f8(qt)
            parts = (qt_hi, qt_hi4, qt_lo)
        else:
            parts = (qt,)
        for hh in range(HEADS_A):
            rows = slice(hh * HEAD_DIM, (hh + 1) * HEAD_DIM)
            for c in range(tm // tq):
                cols = slice(c * tq, (c + 1) * tq)
                for slot, part in enumerate(parts):
                    qat_ref[0, hh, c, slot * HEAD_DIM:(slot + 1) * HEAD_DIM, :] = (
                        part[rows, cols].astype(qat_ref.dtype))

    def finish_kva(h):
        k = norm_rope(h[:, 0:KVD_A], gk_ref[...])
        if use_f8:
            k_hi, k_lo, k_hi4 = _split_f8(k)
        vt = h[:, KVD_A:2 * KVD_A].T
        for g in range(KV_A):
            gl = slice(g * HEAD_DIM, (g + 1) * HEAD_DIM)
            if use_f8:
                ka_ref[0, g] = jnp.concatenate([k_hi[:, gl], k_lo[:, gl], k_hi4[:, gl]], axis=1).astype(F8)
            else:
                ka_ref[0, g] = k[:, gl].astype(BF16)
            vat_ref[0, g, 0, 0:HEAD_DIM, :] = vt[gl].astype(BF16)
            vat_ref[0, g, 0, HEAD_DIM:V_ROWS, :] = ones_rows

    def finish_qb(h):
        qbt = (h * (SM_SCALE * LOG2E)).T
        for hh in range(HEADS_B):
            qbt_ref[0, hh] = qbt[hh * HEAD_DIM:(hh + 1) * HEAD_DIM].astype(BF16)

    def finish_kvb(h):
        kb = h[:, 0:KVD_B]
        vbt = h[:, KVD_B:2 * KVD_B].T
        for g in range(KV_B):
            gl = slice(g * HEAD_DIM, (g + 1) * HEAD_DIM)
            kb_ref[0, g] = kb[:, gl].astype(BF16)
            vbt_ref[0, g, 0:HEAD_DIM, :] = vbt[gl].astype(BF16)
            vbt_ref[0, g, HEAD_DIM:V_ROWS, :] = ones_rows

    c_kva = Q_A
    c_qb = Q_A + 2 * KVD_A
    c_kvb = c_qb + Q_B
    h_qa = project(0, c_kva)
    h_kva = project(c_kva, c_qb)
    finish_qa(h_qa)
    h_qb = project(c_qb, c_kvb)
    finish_kva(h_kva)
    h_kvb = project(c_kvb, c_kvb + 2 * KVD_B)
    finish_qb(h_qb)
    finish_kvb(h_kvb)


def _inproj(x, w, gq, gk, cos_t, sin_t, ones_blk, *, tm, tkc, tq, use_f8):
    B, S, D = x.shape
    ns = S // tm
    n_cols = w.shape[1]
    per_chunk = tkc // tm
    qk_dim = (F8_SLOTS if use_f8 else 1) * HEAD_DIM
    qk_dtype = F8 if use_f8 else BF16
    out_shape = (
        jax.ShapeDtypeStruct((B, HEADS_A, S // tq, qk_dim, tq), qk_dtype),
        jax.ShapeDtypeStruct((B, KV_A, S, qk_dim), qk_dtype),
        jax.ShapeDtypeStruct((B, KV_A, S // tkc, V_ROWS, tkc), BF16),
        jax.ShapeDtypeStruct((B, HEADS_B, HEAD_DIM, S), BF16),
        jax.ShapeDtypeStruct((B, KV_B, S, HEAD_DIM), BF16),
        jax.ShapeDtypeStruct((B, KV_B, V_ROWS, S), BF16),
    )
    return pl.pallas_call(
        functools.partial(_inproj_kernel, use_f8),
        grid=(B, ns),
        in_specs=[
            pl.BlockSpec((1, tm, D), lambda b, s: (b, s, 0)),
            _resident((D, n_cols)),
            _resident((1, Q_A)),
            _resident((1, KVD_A)),
            pl.BlockSpec((tm, LANES), lambda b, s: (s, 0)),
            pl.BlockSpec((tm, LANES), lambda b, s: (s, 0)),
            _resident((LANES, LANES)),
        ],
        out_specs=(
            pl.BlockSpec((1, HEADS_A, tm // tq, qk_dim, tq), lambda b, s: (b, 0, s, 0, 0)),
            pl.BlockSpec((1, KV_A, tm, qk_dim), lambda b, s: (b, 0, s, 0)),
            pl.BlockSpec((1, KV_A, 1, V_ROWS, tm), lambda b, s: (b, 0, s // per_chunk, 0, s % per_chunk)),
            pl.BlockSpec((1, HEADS_B, HEAD_DIM, tm), lambda b, s: (b, 0, 0, s)),
            pl.BlockSpec((1, KV_B, tm, HEAD_DIM), lambda b, s: (b, 0, s, 0)),
            pl.BlockSpec((1, KV_B, V_ROWS, tm), lambda b, s: (b, 0, 0, s)),
        ),
        out_shape=out_shape,
        compiler_params=_cparams(("parallel", "parallel")),
        name="inproj",
    )(x, w, gq, gk, cos_t, sin_t, ones_blk)


def _gattn_kernel(qt_ref, k_ref, vt_ref, o_ref, s_sc, mnew_sc, alpha_sc, m_sc, acc_sc):
    n_chunks, _, tkc = vt_ref.shape[2:]
    nq, _, tq = qt_ref.shape[2:]
    assert n_chunks % 2 == 0 and n_chunks >= 4
    n_pieces = tkc // G_PIECE

    def reset_max():
        m_sc[...] = jnp.full(m_sc.shape, -jnp.inf, F32)

    def score_piece(qi, j, buf, h, pc):
        rows = pl.ds(pl.multiple_of(j * tkc, tkc) + pc * G_PIECE, G_SCORE_PIECES * G_PIECE)
        st = jnp.dot(k_ref[0, 0, rows, :], qt_ref[0, h, qi], preferred_element_type=F32)
        s_sc[buf, h, pc * G_PIECE:(pc + G_SCORE_PIECES) * G_PIECE, :] = st
        return jnp.max(st, axis=0, keepdims=True)

    def value_piece(j, buf, h, pc):
        cols = slice(pc * G_PIECE, (pc + 1) * G_PIECE)
        pt = jnp.exp2(s_sc[buf, h, cols, :] - mnew_sc[buf, h]).astype(BF16)
        return jnp.dot(vt_ref[0, 0, j, :, cols], pt, preferred_element_type=F32)

    def stage(score_of, value_of):
        for h in range(G_A):
            piece_max, pv = [], None
            for pc in range(n_pieces):
                if score_of is not None and pc % G_SCORE_PIECES == 0:
                    piece_max.append(score_piece(*score_of, h, pc))
                if value_of is not None:
                    d = value_piece(*value_of, h, pc)
                    pv = d if pv is None else pv + d
            if score_of is not None:
                buf = score_of[2]
                m_prev = m_sc[h]
                m_new = functools.reduce(jnp.maximum, piece_max, m_prev)
                alpha_sc[buf, h] = jnp.exp2(m_prev - m_new)
                mnew_sc[buf, h] = m_new
                m_sc[h] = m_new
            if value_of is not None:
                acc_sc[h] = alpha_sc[value_of[1], h] * acc_sc[h] + pv

    def middle_chunks(qi):
        def pair(i, carry):
            stage((qi, 2 * i + 1, 1), (2 * i, 0))
            stage((qi, 2 * i + 2, 0), (2 * i + 1, 1))
            return carry

        lax.fori_loop(0, n_chunks // 2 - 1, pair, 0, unroll=True)
        stage((qi, n_chunks - 1, 1), (n_chunks - 2, 0))

    def finalize(qi):
        outs = []
        for h in range(G_A):
            a = acc_sc[h]
            outs.append(a[0:HEAD_DIM] / a[HEAD_DIM:HEAD_DIM + 1])
        rows = pl.ds(pl.multiple_of(qi * tq, tq), tq)
        o_ref[0, rows, :] = jnp.concatenate(outs, axis=0).T.astype(o_ref.dtype)

    acc_sc[...] = jnp.zeros(acc_sc.shape, F32)
    reset_max()
    stage((0, 0, 0), None)

    def q_block(qi, carry):
        middle_chunks(qi)
        reset_max()
        stage((qi + 1, 0, 0), (n_chunks - 1, 1))
        finalize(qi)
        return carry

    lax.fori_loop(0, nq - 1, q_block, 0)
    middle_chunks(nq - 1)
    stage(None, (n_chunks - 1, 1))
    finalize(nq - 1)


def _gattn(qat, ka, vat):
    B, _, nq, qk_dim, tq = qat.shape
    S = nq * tq
    n_chunks, _, tkc = vat.shape[2:]
    return pl.pallas_call(
        _gattn_kernel,
        grid=(B, KV_A),
        in_specs=[
            pl.BlockSpec((1, G_A, nq, qk_dim, tq), lambda b, g: (b, g, 0, 0, 0)),
            pl.BlockSpec((1, 1, S, qk_dim), lambda b, g: (b, g, 0, 0)),
            pl.BlockSpec((1, 1, n_chunks, V_ROWS, tkc), lambda b, g: (b, g, 0, 0, 0)),
        ],
        out_specs=pl.BlockSpec((1, S, G_A * HEAD_DIM), lambda b, g: (b, 0, g)),
        out_shape=jax.ShapeDtypeStruct((B, S, Q_A), BF16),
        scratch_shapes=[
            pltpu.VMEM((2, G_A, tkc, tq), F32),
            pltpu.VMEM((2, G_A, 1, tq), F32),
            pltpu.VMEM((2, G_A, 1, tq), F32),
            pltpu.VMEM((G_A, 1, tq), F32),
            pltpu.VMEM((G_A, V_ROWS, tq), F32),
        ],
        compiler_params=_cparams(("parallel", "parallel")),
        name="gattn",
    )(qat, ka, vat)


W_TQ = 2 * Q_BLOCK
W_TK = W_TQ + 2 * WINDOW
W_TILES = 2
W_TILE_BLKS = W_TK // Q_BLOCK
W_NBLK = (W_TILES * W_TQ + 2 * WINDOW) // Q_BLOCK


def _wattn_kernel(rb_ref, sink_ref, bucket_ref, qt_ref, *refs):
    k_refs = refs[:W_NBLK]
    v_refs = refs[W_NBLK:2 * W_NBLK]
    o_ref, bias_sc, s_sc = refs[2 * W_NBLK:]
    b = pl.program_id(0)
    i = pl.program_id(1)
    n = pl.num_programs(1)

    @pl.when((b == 0) & (i == 0))
    def _():
        bucket = bucket_ref[...]
        krow = lax.broadcasted_iota(jnp.int32, (W_TK, W_TQ), 0)
        qcol = lax.broadcasted_iota(jnp.int32, (W_TK, W_TQ), 1)
        in_window = jnp.abs(krow - WINDOW - qcol) <= WINDOW
        for h in range(HEADS_B):
            t = jnp.zeros((W_TK, W_TQ), F32)
            for bk in range(N_BUCKETS):
                t = jnp.where(bucket == bk, rb_ref[bk, h], t)
            t = jnp.where(in_window, t * LOG2E, -jnp.inf)
            bias_sc[0, h] = jnp.where(krow >= Q_BLOCK, t, -jnp.inf)
            bias_sc[1, h] = t
            bias_sc[2, h] = jnp.where(krow < W_TK - Q_BLOCK, t, -jnp.inf)

    def table_of(t):
        table = 1
        if t == 0:
            table = jnp.where(i == 0, 0, table)
        if t == W_TILES - 1:
            table = jnp.where(i == n - 1, 2, table)
        return table

    tables = [table_of(t) for t in range(W_TILES)]
    tile_blks = [slice(t * (W_TQ // Q_BLOCK), t * (W_TQ // Q_BLOCK) + W_TILE_BLKS) for t in range(W_TILES)]
    kws = [[jnp.concatenate([r[0, g] for r in k_refs[tile_blks[t]]], axis=0) for g in range(KV_B)]
           for t in range(W_TILES)]
    vws = [[jnp.concatenate([r[0, g] for r in v_refs[tile_blks[t]]], axis=1) for g in range(KV_B)]
           for t in range(W_TILES)]
    sinks = [sink_ref[h] * LOG2E for h in range(HEADS_B)]
    items = [(t, h) for t in range(W_TILES) for h in range(HEADS_B)]
    maxes = [None] * len(items)

    def scores(n_item):
        t, h = items[n_item]
        qt = qt_ref[0, h, :, t * W_TQ:(t + 1) * W_TQ]
        st = jnp.dot(kws[t][h // G_B], qt, preferred_element_type=F32) + bias_sc[tables[t], h]
        s_sc[n_item] = st
        maxes[n_item] = jnp.maximum(jnp.max(st, axis=0, keepdims=True), sinks[h])

    def finish(n_item):
        t, h = items[n_item]
        pt = jnp.exp2(s_sc[n_item] - maxes[n_item]).astype(BF16)
        acc = jnp.dot(vws[t][h // G_B], pt, preferred_element_type=F32)
        denom = acc[HEAD_DIM:HEAD_DIM + 1] + jnp.exp2(sinks[h] - maxes[n_item])
        return acc[0:HEAD_DIM] / denom

    ahead = 4
    for n_item in range(ahead):
        scores(n_item)
    outs = []
    for n_item in range(len(items)):
        if n_item + ahead < len(items):
            scores(n_item + ahead)
        outs.append(finish(n_item))
        if len(outs) == HEADS_B:
            t = items[n_item][0]
            o_ref[0, t * W_TQ:(t + 1) * W_TQ, :] = jnp.concatenate(outs, axis=0).T.astype(o_ref.dtype)
            outs = []


def _wattn(rel_bias, sink, bucket, qbt, kb, vbt):
    B, _, _, S = qbt.shape
    step_q = W_TILES * W_TQ
    n = S // step_q
    last_blk = S // Q_BLOCK - 1
    smem = pl.BlockSpec(memory_space=pltpu.SMEM)

    def blk(c):
        return lambda b, i: jnp.clip(i * (step_q // Q_BLOCK) - 1 + c, 0, last_blk)

    k_specs = [pl.BlockSpec((1, KV_B, Q_BLOCK, HEAD_DIM), lambda b, i, f=blk(c): (b, 0, f(b, i), 0))
               for c in range(W_NBLK)]
    v_specs = [pl.BlockSpec((1, KV_B, V_ROWS, Q_BLOCK), lambda b, i, f=blk(c): (b, 0, 0, f(b, i)))
               for c in range(W_NBLK)]
    return pl.pallas_call(
        _wattn_kernel,
        grid=(B, n),
        in_specs=[smem, smem, _resident((W_TK, W_TQ)),
                  pl.BlockSpec((1, HEADS_B, HEAD_DIM, step_q), lambda b, i: (b, 0, 0, i))] + k_specs + v_specs,
        out_specs=pl.BlockSpec((1, step_q, Q_B), lambda b, i: (b, i, 0)),
        out_shape=jax.ShapeDtypeStruct((B, S, Q_B), BF16),
        scratch_shapes=[pltpu.VMEM((3, HEADS_B, W_TK, W_TQ), F32),
                        pltpu.VMEM((W_TILES * HEADS_B, W_TK, W_TQ), F32)],
        compiler_params=_cparams(("arbitrary", "arbitrary")),
        name="wattn",
    )(rel_bias, sink, bucket, qbt, *([kb] * W_NBLK), *([vbt] * W_NBLK))


def _layer_norm(y, g, b):
    mu = jnp.mean(y, axis=-1, keepdims=True)
    d = y - mu
    var = jnp.mean(d * d, axis=-1, keepdims=True)
    return d * lax.rsqrt(var + LN_EPS) * g + b


def _rms(y, g):
    yf = y.astype(F32)
    return yf * lax.rsqrt(jnp.mean(yf * yf, axis=-1, keepdims=True) + RMS_EPS) * g


def _outproj_kernel(alpha, ya_ref, yb_ref, x_ref, ga_ref, gb_ref, wa_ref, wb_ref, lg_ref, lb_ref, o_ref):
    na = _rms(ya_ref[0], ga_ref[...]).astype(BF16)
    nb = _rms(yb_ref[0], gb_ref[...]).astype(BF16)
    mix = (jnp.dot(na, wa_ref[...], preferred_element_type=F32)
           + jnp.dot(nb, wb_ref[...], preferred_element_type=F32))
    o_ref[0] = _layer_norm(alpha * x_ref[0] + mix, lg_ref[...], lb_ref[...])


def _outproj(ya, yb, x, ga, gb, wa, wb, lg, lb, *, alpha, tm):
    B, S, D = x.shape
    row = lambda w: pl.BlockSpec((1, tm, w), lambda b, s: (b, s, 0))
    return pl.pallas_call(
        functools.partial(_outproj_kernel, alpha),
        grid=(B, S // tm),
        in_specs=[row(Q_A), row(Q_B), row(D), _resident((1, Q_A)), _resident((1, Q_B)),
                  _resident((Q_A, D)), _resident((Q_B, D)), _resident((1, D)), _resident((1, D))],
        out_specs=row(D),
        out_shape=jax.ShapeDtypeStruct((B, S, D), F32),
        compiler_params=_cparams(("parallel", "parallel")),
        name="outproj",
    )(ya, yb, x, ga, gb, wa, wb, lg, lb)


def _gelu_tanh(x):
    return 0.5 * x * (1.0 + jnp.tanh(math.sqrt(2.0 / math.pi) * (x + 0.044715 * (x * x * x))))


def _ffn_kernel(alpha, ff_chunk, x_ref, xp_ref, xn_ref, wg_ref, wu_ref, cw_ref, cb_ref, wd_ref,
                lg_ref, lb_ref, o_ref, h_sc):
    s = pl.program_id(1)
    ns = pl.num_programs(1)
    tm = x_ref.shape[1]
    d_ff = wg_ref.shape[1]
    halo = xp_ref.shape[1]
    x = x_ref[0]
    xb = x.astype(BF16)
    xh = jnp.concatenate([xp_ref[0], xn_ref[0]], axis=0).astype(BF16)
    has_prev = (s > 0).astype(F32)
    has_next = (s < ns - 1).astype(F32)
    row = lax.broadcasted_iota(jnp.int32, (tm, 1), 0)
    for c in range(d_ff // ff_chunk):
        cs = slice(c * ff_chunk, (c + 1) * ff_chunk)
        wg = wg_ref[:, cs]
        g = jnp.dot(xb, wg, preferred_element_type=F32)
        u = jnp.dot(xb, wu_ref[:, cs], preferred_element_type=F32)
        gh = jnp.dot(xh, wg, preferred_element_type=F32)
        g_before = gh[halo - 1:halo] * has_prev
        g_after = gh[halo:halo + 1] * has_next
        g_m1 = jnp.where(row == 0, g_before, pltpu.roll(g, 1, 0))
        g_p1 = jnp.where(row == tm - 1, g_after, pltpu.roll(g, tm - 1, 0))
        gc = cb_ref[:, cs] + g_m1 * cw_ref[0:1, cs] + g * cw_ref[1:2, cs] + g_p1 * cw_ref[2:3, cs]
        h_sc[:, cs] = (_gelu_tanh(gc) * u).astype(BF16)
    ffn = jnp.dot(h_sc[...], wd_ref[...], preferred_element_type=F32)
    o_ref[0] = _layer_norm(alpha * x + ffn, lg_ref[...], lb_ref[...])


def _ffn(x, wg, wu, cw, cb, wd, lg, lb, *, alpha, tm, ff_chunk, halo=8):
    B, S, D = x.shape
    d_ff = wg.shape[1]
    ns = S // tm
    hb = tm // halo
    last = S // halo - 1
    return pl.pallas_call(
        functools.partial(_ffn_kernel, alpha, ff_chunk),
        grid=(B, ns),
        in_specs=[
            pl.BlockSpec((1, tm, D), lambda b, s: (b, s, 0)),
            pl.BlockSpec((1, halo, D), lambda b, s: (b, jnp.maximum(s * hb - 1, 0), 0)),
            pl.BlockSpec((1, halo, D), lambda b, s: (b, jnp.minimum((s + 1) * hb, last), 0)),
            _resident((D, d_ff)), _resident((D, d_ff)), _resident((CONV_W, d_ff)), _resident((1, d_ff)),
            _resident((d_ff, D)), _resident((1, D)), _resident((1, D)),
        ],
        out_specs=pl.BlockSpec((1, tm, D), lambda b, s: (b, s, 0)),
        out_shape=jax.ShapeDtypeStruct((B, S, D), F32),
        scratch_shapes=[pltpu.VMEM((tm, d_ff), BF16)],
        compiler_params=_cparams(("parallel", "parallel")),
        name="ffn",
    )(x, x, x, wg, wu, cw, cb, wd, lg, lb)


def _rope_tables(seq_len):
    rows_n = seq_len // GRID_W
    row = jnp.repeat(jnp.arange(rows_n, dtype=F32), GRID_W)
    col = jnp.tile(jnp.arange(GRID_W, dtype=F32), rows_n)
    half = HEAD_DIM // 2
    inv_freq = ROPE_THETA ** (-jnp.arange(0, half, 2, dtype=F32) / half)
    ang = jnp.concatenate([row[:, None] * inv_freq, col[:, None] * inv_freq], axis=-1)
    c, s = jnp.cos(ang), jnp.sin(ang)
    return jnp.concatenate([c, c, c, c], axis=-1), jnp.concatenate([-s, s, -s, s], axis=-1)


def _t5_bucket(rel):
    half = N_BUCKETS // 2
    max_exact = half // 2
    bucket = jnp.where(rel > 0, half, 0)
    rp = jnp.abs(rel)
    rpf = jnp.maximum(rp, 1).astype(F32)
    large = max_exact + (jnp.log(rpf / max_exact) / math.log(MAX_DISTANCE / max_exact)
                         * (half - max_exact)).astype(jnp.int32)
    large = jnp.minimum(large, half - 1)
    return bucket + jnp.where(rp < max_exact, rp, large)


def _deinterleave_perm(n_heads):
    per_head = np.concatenate([np.arange(0, HEAD_DIM, 2), np.arange(1, HEAD_DIM, 2)])
    return np.concatenate([h * HEAD_DIM + per_head for h in range(n_heads)])


def kernel(x, rel_bias, w_in, q_norm, k_norm, sink, out_norm_a, out_norm_b, w_out,
           ln1_g, ln1_b, w_gate, w_up, conv_w, conv_b, w_down, ln2_g, ln2_b):
    B, S, D = x.shape
    depth = w_in.shape[0]
    alpha = (2.0 * depth) ** 0.25

    cos_t, sin_t = _rope_tables(S)
    qpos = jnp.arange(W_TQ, dtype=jnp.int32)
    kpos = jnp.arange(W_TK, dtype=jnp.int32) - WINDOW
    bucket = _t5_bucket(kpos[:, None] - qpos[None, :]).astype(jnp.int32)
    blk = np.arange(LANES) // HEAD_DIM
    ones_blk = jnp.asarray(blk[:, None] == blk[None, :], dtype=BF16)

    perm_q = _deinterleave_perm(HEADS_A)
    perm_k = _deinterleave_perm(KV_A)
    col_perm = np.concatenate([perm_q, Q_A + perm_k, np.arange(Q_A + KVD_A, w_in.shape[2])])
    head_perm = _deinterleave_perm(1)

    for l in range(depth):
        w = w_in[l][:, col_perm].astype(BF16)
        gq_max = jnp.max(jnp.abs(q_norm[l])) * (SM_SCALE * LOG2E)
        gk_max = jnp.max(jnp.abs(k_norm[l]))
        bal = jnp.exp2(jnp.round(0.5 * jnp.log2(jnp.maximum(gq_max, 1e-30) / jnp.maximum(gk_max, 1e-30))))
        gq = jnp.tile(q_norm[l][head_perm] * (SM_SCALE * LOG2E) / bal, HEADS_A)[None, :]
        gk = jnp.tile(k_norm[l][head_perm] * bal, KV_A)[None, :]

        def front(use_f8, x, w, gq, gk):
            qat, ka, vat, qbt, kb, vbt = _inproj(x, w, gq, gk, cos_t, sin_t, ones_blk,
                                                 tm=ROW_TILE, tkc=G_CHUNK, tq=G_TQ, use_f8=use_f8)
            return _gattn(qat, ka, vat), qbt, kb, vbt

        bound = math.sqrt(HEAD_DIM) * jnp.maximum(gk_max * bal, gq_max / bal)
        ya, qbt, kb, vbt = lax.cond(bound <= F8_SAFE_MAX, functools.partial(front, True),
                                    functools.partial(front, False), x, w, gq, gk)
        yb = _wattn(rel_bias, sink[l], bucket, qbt, kb, vbt)
        x = _outproj(ya, yb, x, out_norm_a[l][None, :], out_norm_b[l][None, :],
                     w_out[l][:Q_A].astype(BF16), w_out[l][Q_A:].astype(BF16),
                     ln1_g[l][None, :], ln1_b[l][None, :], alpha=alpha, tm=ROW_TILE)
        x = _ffn(x, w_gate[l].astype(BF16), w_up[l].astype(BF16), conv_w[l], conv_b[l][None, :],
                 w_down[l].astype(BF16), ln2_g[l][None, :], ln2_b[l][None, :],
                 alpha=alpha, tm=ROW_TILE, ff_chunk=FF_CHUNK)
    return x
```

```python
import functools
import math

import jax
import jax.numpy as jnp
import numpy as np
from jax import lax
from jax.experimental import pallas as pl
from jax.experimental.pallas import tpu as pltpu

F32 = jnp.float32
BF16 = jnp.bfloat16
F8 = jnp.float8_e4m3fn
F8_SLOTS = 3
F8_SAFE_MAX = 384.0

HEAD_DIM = 64
HEADS_A = 8
HEADS_B = 8
KV_A = 2
KV_B = 2
G_A = HEADS_A // KV_A
G_B = HEADS_B // KV_B
Q_A = HEADS_A * HEAD_DIM
KVD_A = KV_A * HEAD_DIM
Q_B = HEADS_B * HEAD_DIM
KVD_B = KV_B * HEAD_DIM
GRID_W = 64
ROPE_THETA = 10000.0
Q_BLOCK = 128
WINDOW = 128
N_BUCKETS = 32
MAX_DISTANCE = 128
CONV_W = 3
RMS_EPS = 1e-6
LN_EPS = 1e-5
LOG2E = math.log2(math.e)
SM_SCALE = HEAD_DIM ** -0.5

LANES = 128
V_ROWS = 80
MXU_TILE = 256
ROW_TILE = 1024
FF_CHUNK = MXU_TILE
G_TQ = MXU_TILE
G_CHUNK = 1024
G_PIECE = MXU_TILE
G_SCORE_PIECES = G_CHUNK // G_PIECE
VMEM_LIMIT = 56 * 1024 * 1024


def _cparams(sem):
    return pltpu.CompilerParams(dimension_semantics=sem, vmem_limit_bytes=VMEM_LIMIT)


def _resident(shape):
    nd = len(shape)
    return pl.BlockSpec(shape, lambda *_: (0,) * nd, pipeline_mode=pl.Buffered(1))


def _split_f8(x):
    hi = x.astype(F8).astype(F32)
    lo = ((x - hi) * 4.0).astype(F8).astype(F32)
    hi4 = (hi * 0.25).astype(F8).astype(F32)
    return hi, lo, hi4


def _inproj_kernel(use_f8, x_ref, w_ref, gq_ref, gk_ref, cos_ref, sin_ref, j_ref,
                   qat_ref, ka_ref, vat_ref, qbt_ref, kb_ref, vbt_ref):
    tm = x_ref.shape[1]
    tq = qat_ref.shape[4]
    x = x_ref[0].astype(BF16)

    lane = lax.broadcasted_iota(jnp.int32, (tm, LANES), 1)
    first_half = (lane % HEAD_DIM) < (HEAD_DIM // 2)
    row = lax.broadcasted_iota(jnp.int32, (V_ROWS - HEAD_DIM, tm), 0)
    ones_rows = jnp.where(row == 0, 1.0, 0.0).astype(BF16)

    def project(c0, c1):
        return jnp.dot(x, w_ref[:, c0:c1], preferred_element_type=F32)

    def norm_rope(hc, gain):
        ssq = jnp.dot((hc * hc).astype(BF16), j_ref[...], preferred_element_type=F32)
        y = hc * lax.rsqrt(ssq * (1.0 / HEAD_DIM) + RMS_EPS) * gain
        partner = jnp.where(first_half, pltpu.roll(y, LANES - HEAD_DIM // 2, 1),
                            pltpu.roll(y, HEAD_DIM // 2, 1))
        return y * cos_ref[...] + partner * sin_ref[...]

    def finish_qa(h):
        q = jnp.concatenate(
            [norm_rope(h[:, c * LANES:(c + 1) * LANES], gq_ref[:, c * LANES:(c + 1) * LANES])
             for c in range(Q_A // LANES)], axis=1)
        qt = q.T
        if use_f8:
            qt_hi, qt_lo, qt_hi4 = _split_f8(qt)
            parts = (qt_hi, qt_hi4, qt_lo)
        else:
            parts = (qt,)
        for hh in range(HEADS_A):
            rows = slice(hh * HEAD_DIM, (hh + 1) * HEAD_DIM)
            for c in range(tm // tq):
                cols = slice(c * tq, (c + 1) * tq)
                for slot, part in enumerate(parts):
                    qat_ref[0, hh, c, slot * HEAD_DIM:(slot + 1) * HEAD_DIM, :] = (
                        part[rows, cols].astype(qat_ref.dtype))

    def finish_kva(h):
        k = norm_rope(h[:, 0:KVD_A], gk_ref[...])
        if use_f8:
            k_hi, k_lo, k_hi4 = _split_f8(k)
        vt = h[:, KVD_A:2 * KVD_A].T
        for g in range(KV_A):
            gl = slice(g * HEAD_DIM, (g + 1) * HEAD_DIM)
            if use_f8:
                ka_ref[0, g] = jnp.concatenate([k_hi[:, gl], k_lo[:, gl], k_hi4[:, gl]], axis=1).astype(F8)
            else:
                ka_ref[0, g] = k[:, gl].astype(BF16)
            vat_ref[0, g, 0, 0:HEAD_DIM, :] = vt[gl].astype(BF16)
            vat_ref[0, g, 0, HEAD_DIM:V_ROWS, :] = ones_rows

    def finish_qb(h):
        qbt = (h * (SM_SCALE * LOG2E)).T
        for hh in range(HEADS_B):
            qbt_ref[0, hh] = qbt[hh * HEAD_DIM:(hh + 1) * HEAD_DIM].astype(BF16)

    def finish_kvb(h):
        kb = h[:, 0:KVD_B]
        vbt = h[:, KVD_B:2 * KVD_B].T
        for g in range(KV_B):
            gl = slice(g * HEAD_DIM, (g + 1) * HEAD_DIM)
            kb_ref[0, g] = kb[:, gl].astype(BF16)
            vbt_ref[0, g, 0:HEAD_DIM, :] = vbt[gl].astype(BF16)
            vbt_ref[0, g, HEAD_DIM:V_ROWS, :] = ones_rows

    c_kva = Q_A
    c_qb = Q_A + 2 * KVD_A
    c_kvb = c_qb + Q_B
    h_qa = project(0, c_kva)
    h_kva = project(c_kva, c_qb)
    finish_qa(h_qa)
    h_qb = project(c_qb, c_kvb)
    finish_kva(h_kva)
    h_kvb = project(c_kvb, c_kvb + 2 * KVD_B)
    finish_qb(h_qb)
    finish_kvb(h_kvb)


def _inproj(x, w, gq, gk, cos_t, sin_t, ones_blk, *, tm, tkc, tq, use_f8):
    B, S, D = x.shape
    ns = S // tm
    n_cols = w.shape[1]
    per_chunk = tkc // tm
    qk_dim = (F8_SLOTS if use_f8 else 1) * HEAD_DIM
    qk_dtype = F8 if use_f8 else BF16
    out_shape = (
        jax.ShapeDtypeStruct((B, HEADS_A, S // tq, qk_dim, tq), qk_dtype),
        jax.ShapeDtypeStruct((B, KV_A, S, qk_dim), qk_dtype),
        jax.ShapeDtypeStruct((B, KV_A, S // tkc, V_ROWS, tkc), BF16),
        jax.ShapeDtypeStruct((B, HEADS_B, HEAD_DIM, S), BF16),
        jax.ShapeDtypeStruct((B, KV_B, S, HEAD_DIM), BF16),
        jax.ShapeDtypeStruct((B, KV_B, V_ROWS, S), BF16),
    )
    return pl.pallas_call(
        functools.partial(_inproj_kernel, use_f8),
        grid=(B, ns),
        in_specs=[
            pl.BlockSpec((1, tm, D), lambda b, s: (b, s, 0)),
            _resident((D, n_cols)),
            _resident((1, Q_A)),
            _resident((1, KVD_A)),
            pl.BlockSpec((tm, LANES), lambda b, s: (s, 0)),
            pl.BlockSpec((tm, LANES), lambda b, s: (s, 0)),
            _resident((LANES, LANES)),
        ],
        out_specs=(
            pl.BlockSpec((1, HEADS_A, tm // tq, qk_dim, tq), lambda b, s: (b, 0, s, 0, 0)),
            pl.BlockSpec((1, KV_A, tm, qk_dim), lambda b, s: (b, 0, s, 0)),
            pl.BlockSpec((1, KV_A, 1, V_ROWS, tm), lambda b, s: (b, 0, s // per_chunk, 0, s % per_chunk)),
            pl.BlockSpec((1, HEADS_B, HEAD_DIM, tm), lambda b, s: (b, 0, 0, s)),
            pl.BlockSpec((1, KV_B, tm, HEAD_DIM), lambda b, s: (b, 0, s, 0)),
            pl.BlockSpec((1, KV_B, V_ROWS, tm), lambda b, s: (b, 0, 0, s)),
        ),
        out_shape=out_shape,
        compiler_params=_cparams(("parallel", "parallel")),
        name="inproj",
    )(x, w, gq, gk, cos_t, sin_t, ones_blk)


def _gattn_kernel(qt_ref, k_ref, vt_ref, o_ref, s_sc, mnew_sc, alpha_sc, m_sc, acc_sc):
    n_chunks, _, tkc = vt_ref.shape[2:]
    nq, _, tq = qt_ref.shape[2:]
    assert n_chunks % 2 == 0 and n_chunks >= 4
    n_pieces = tkc // G_PIECE

    def reset_max():
        m_sc[...] = jnp.full(m_sc.shape, -jnp.inf, F32)

    def score_piece(qi, j, buf, h, pc):
        rows = pl.ds(pl.multiple_of(j * tkc, tkc) + pc * G_PIECE, G_SCORE_PIECES * G_PIECE)
        st = jnp.dot(k_ref[0, 0, rows, :], qt_ref[0, h, qi], preferred_element_type=F32)
        s_sc[buf, h, pc * G_PIECE:(pc + G_SCORE_PIECES) * G_PIECE, :] = st
        return jnp.max(st, axis=0, keepdims=True)

    def value_piece(j, buf, h, pc):
        cols = slice(pc * G_PIECE, (pc + 1) * G_PIECE)
        pt = jnp.exp2(s_sc[buf, h, cols, :] - mnew_sc[buf, h]).astype(BF16)
        return jnp.dot(vt_ref[0, 0, j, :, cols], pt, preferred_element_type=F32)

    def stage(score_of, value_of):
        for h in range(G_A):
            piece_max, pv = [], None
            for pc in range(n_pieces):
                if score_of is not None and pc % G_SCORE_PIECES == 0:
                    piece_max.append(score_piece(*score_of, h, pc))
                if value_of is not None:
                    d = value_piece(*value_of, h, pc)
                    pv = d if pv is None else pv + d
            if score_of is not None:
                buf = score_of[2]
                m_prev = m_sc[h]
                m_new = functools.reduce(jnp.maximum, piece_max, m_prev)
                alpha_sc[buf, h] = jnp.exp2(m_prev - m_new)
                mnew_sc[buf, h] = m_new
                m_sc[h] = m_new
            if value_of is not None:
                acc_sc[h] = alpha_sc[value_of[1], h] * acc_sc[h] + pv

    def middle_chunks(qi):
        def pair(i, carry):
            stage((qi, 2 * i + 1, 1), (2 * i, 0))
            stage((qi, 2 * i + 2, 0), (2 * i + 1, 1))
            return carry

        lax.fori_loop(0, n_chunks // 2 - 1, pair, 0, unroll=True)
        stage((qi, n_chunks - 1, 1), (n_chunks - 2, 0))

    def finalize(qi):
        outs = []
        for h in range(G_A):
            a = acc_sc[h]
            outs.append(a[0:HEAD_DIM] / a[HEAD_DIM:HEAD_DIM + 1])
        rows = pl.ds(pl.multiple_of(qi * tq, tq), tq)
        o_ref[0, rows, :] = jnp.concatenate(outs, axis=0).T.astype(o_ref.dtype)

    acc_sc[...] = jnp.zeros(acc_sc.shape, F32)
    reset_max()
    stage((0, 0, 0), None)

    def q_block(qi, carry):
        middle_chunks(qi)
        reset_max()
        stage((qi + 1, 0, 0), (n_chunks - 1, 1))
        finalize(qi)
        return carry

    lax.fori_loop(0, nq - 1, q_block, 0)
    middle_chunks(nq - 1)
    stage(None, (n_chunks - 1, 1))
    finalize(nq - 1)


def _gattn(qat, ka, vat):
    B, _, nq, qk_dim, tq = qat.shape
    S = nq * tq
    n_chunks, _, tkc = vat.shape[2:]
    return pl.pallas_call(
        _gattn_kernel,
        grid=(B, KV_A),
        in_specs=[
            pl.BlockSpec((1, G_A, nq, qk_dim, tq), lambda b, g: (b, g, 0, 0, 0)),
            pl.BlockSpec((1, 1, S, qk_dim), lambda b, g: (b, g, 0, 0)),
            pl.BlockSpec((1, 1, n_chunks, V_ROWS, tkc), lambda b, g: (b, g, 0, 0, 0)),
        ],
        out_specs=pl.BlockSpec((1, S, G_A * HEAD_DIM), lambda b, g: (b, 0, g)),
        out_shape=jax.ShapeDtypeStruct((B, S, Q_A), BF16),
        scratch_shapes=[
            pltpu.VMEM((2, G_A, tkc, tq), F32),
            pltpu.VMEM((2, G_A, 1, tq), F32),
            pltpu.VMEM((2, G_A, 1, tq), F32),
            pltpu.VMEM((G_A, 1, tq), F32),
            pltpu.VMEM((G_A, V_ROWS, tq), F32),
        ],
        compiler_params=_cparams(("parallel", "parallel")),
        name="gattn",
    )(qat, ka, vat)


W_TQ = 2 * Q_BLOCK
W_TK = W_TQ + 2 * WINDOW
W_TILES = 4
W_TILE_BLKS = W_TK // Q_BLOCK
W_NBLK = (W_TILES * W_TQ + 2 * WINDOW) // Q_BLOCK


def _wattn_kernel(rb_ref, sink_ref, bucket_ref, qt_ref, *refs):
    k_refs = refs[:W_NBLK]
    v_refs = refs[W_NBLK:2 * W_NBLK]
    o_ref, bias_sc, s_sc = refs[2 * W_NBLK:]
    b = pl.program_id(0)
    i = pl.program_id(1)
    n = pl.num_programs(1)

    @pl.when((b == 0) & (i == 0))
    def _():
        bucket = bucket_ref[...]
        krow = lax.broadcasted_iota(jnp.int32, (W_TK, W_TQ), 0)
        qcol = lax.broadcasted_iota(jnp.int32, (W_TK, W_TQ), 1)
        in_window = jnp.abs(krow - WINDOW - qcol) <= WINDOW
        for h in range(HEADS_B):
            t = jnp.zeros((W_TK, W_TQ), F32)
            for bk in range(N_BUCKETS):
                t = jnp.where(bucket == bk, rb_ref[bk, h], t)
            t = jnp.where(in_window, t * LOG2E, -jnp.inf)
            bias_sc[0, h] = jnp.where(krow >= Q_BLOCK, t, -jnp.inf)
            bias_sc[1, h] = t
            bias_sc[2, h] = jnp.where(krow < W_TK - Q_BLOCK, t, -jnp.inf)

    def table_of(t):
        table = 1
        if t == 0:
            table = jnp.where(i == 0, 0, table)
        if t == W_TILES - 1:
            table = jnp.where(i == n - 1, 2, table)
        return table

    tables = [table_of(t) for t in range(W_TILES)]
    tile_blks = [slice(t * (W_TQ // Q_BLOCK), t * (W_TQ // Q_BLOCK) + W_TILE_BLKS) for t in range(W_TILES)]
    kws = [[jnp.concatenate([r[0, g] for r in k_refs[tile_blks[t]]], axis=0) for g in range(KV_B)]
           for t in range(W_TILES)]
    vws = [[jnp.concatenate([r[0, g] for r in v_refs[tile_blks[t]]], axis=1) for g in range(KV_B)]
           for t in range(W_TILES)]
    sinks = [sink_ref[h] * LOG2E for h in range(HEADS_B)]
    items = [(t, h) for t in range(W_TILES) for h in range(HEADS_B)]
    maxes = [None] * len(items)

    def scores(n_item):
        t, h = items[n_item]
        qt = qt_ref[0, h, :, t * W_TQ:(t + 1) * W_TQ]
        st = jnp.dot(kws[t][h // G_B], qt, preferred_element_type=F32) + bias_sc[tables[t], h]
        s_sc[n_item] = st
        maxes[n_item] = jnp.maximum(jnp.max(st, axis=0, keepdims=True), sinks[h])

    def finish(n_item):
        t, h = items[n_item]
        pt = jnp.exp2(s_sc[n_item] - maxes[n_item]).astype(BF16)
        acc = jnp.dot(vws[t][h // G_B], pt, preferred_element_type=F32)
        denom = acc[HEAD_DIM:HEAD_DIM + 1] + jnp.exp2(sinks[h] - maxes[n_item])
        return acc[0:HEAD_DIM] / denom

    ahead = 4
    for n_item in range(ahead):
        scores(n_item)
    outs = []
    for n_item in range(len(items)):
        if n_item + ahead < len(items):
            scores(n_item + ahead)
        outs.append(finish(n_item))
        if len(outs) == HEADS_B:
            t = items[n_item][0]
            o_ref[0, t * W_TQ:(t + 1) * W_TQ, :] = jnp.concatenate(outs, axis=0).T.astype(o_ref.dtype)
            outs = []


def _wattn(rel_bias, sink, bucket, qbt, kb, vbt):
    B, _, _, S = qbt.shape
    step_q = W_TILES * W_TQ
    n = S // step_q
    last_blk = S // Q_BLOCK - 1
    smem = pl.BlockSpec(memory_space=pltpu.SMEM)

    def blk(c):
        return lambda b, i: jnp.clip(i * (step_q // Q_BLOCK) - 1 + c, 0, last_blk)

    k_specs = [pl.BlockSpec((1, KV_B, Q_BLOCK, HEAD_DIM), lambda b, i, f=blk(c): (b, 0, f(b, i), 0))
               for c in range(W_NBLK)]
    v_specs = [pl.BlockSpec((1, KV_B, V_ROWS, Q_BLOCK), lambda b, i, f=blk(c): (b, 0, 0, f(b, i)))
               for c in range(W_NBLK)]
    return pl.pallas_call(
        _wattn_kernel,
        grid=(B, n),
        in_specs=[smem, smem, _resident((W_TK, W_TQ)),
                  pl.BlockSpec((1, HEADS_B, HEAD_DIM, step_q), lambda b, i: (b, 0, 0, i))] + k_specs + v_specs,
        out_specs=pl.BlockSpec((1, step_q, Q_B), lambda b, i: (b, i, 0)),
        out_shape=jax.ShapeDtypeStruct((B, S, Q_B), BF16),
        scratch_shapes=[pltpu.VMEM((3, HEADS_B, W_TK, W_TQ), F32),
                        pltpu.VMEM((W_TILES * HEADS_B, W_TK, W_TQ), F32)],
        compiler_params=_cparams(("arbitrary", "arbitrary")),
        name="wattn",
    )(rel_bias, sink, bucket, qbt, *([kb] * W_NBLK), *([vbt] * W_NBLK))


def _layer_norm(y, g, b):
    mu = jnp.mean(y, axis=-1, keepdims=True)
    d = y - mu
    var = jnp.mean(d * d, axis=-1, keepdims=True)
    return d * lax.rsqrt(var + LN_EPS) * g + b


def _rms(y, g):
    yf = y.astype(F32)
    return yf * lax.rsqrt(jnp.mean(yf * yf, axis=-1, keepdims=True) + RMS_EPS) * g


def _outproj_kernel(alpha, ya_ref, yb_ref, x_ref, ga_ref, gb_ref, wa_ref, wb_ref, lg_ref, lb_ref, o_ref):
    na = _rms(ya_ref[0], ga_ref[...]).astype(BF16)
    nb = _rms(yb_ref[0], gb_ref[...]).astype(BF16)
    mix = (jnp.dot(na, wa_ref[...], preferred_element_type=F32)
           + jnp.dot(nb, wb_ref[...], preferred_element_type=F32))
    o_ref[0] = _layer_norm(alpha * x_ref[0] + mix, lg_ref[...], lb_ref[...])


def _outproj(ya, yb, x, ga, gb, wa, wb, lg, lb, *, alpha, tm):
    B, S, D = x.shape
    row = lambda w: pl.BlockSpec((1, tm, w), lambda b, s: (b, s, 0))
    return pl.pallas_call(
        functools.partial(_outproj_kernel, alpha),
        grid=(B, S // tm),
        in_specs=[row(Q_A), row(Q_B), row(D), _resident((1, Q_A)), _resident((1, Q_B)),
                  _resident((Q_A, D)), _resident((Q_B, D)), _resident((1, D)), _resident((1, D))],
        out_specs=row(D),
        out_shape=jax.ShapeDtypeStruct((B, S, D), F32),
        compiler_params=_cparams(("parallel", "parallel")),
        name="outproj",
    )(ya, yb, x, ga, gb, wa, wb, lg, lb)


def _gelu_tanh(x):
    return 0.5 * x * (1.0 + jnp.tanh(math.sqrt(2.0 / math.pi) * (x + 0.044715 * (x * x * x))))


def _ffn_kernel(alpha, ff_chunk, x_ref, xp_ref, xn_ref, wg_ref, wu_ref, cw_ref, cb_ref, wd_ref,
                lg_ref, lb_ref, o_ref, h_sc):
    s = pl.program_id(1)
    ns = pl.num_programs(1)
    tm = x_ref.shape[1]
    d_ff = wg_ref.shape[1]
    halo = xp_ref.shape[1]
    x = x_ref[0]
    xb = x.astype(BF16)
    xh = jnp.concatenate([xp_ref[0], xn_ref[0]], axis=0).astype(BF16)
    has_prev = (s > 0).astype(F32)
    has_next = (s < ns - 1).astype(F32)
    row = lax.broadcasted_iota(jnp.int32, (tm, 1), 0)
    for c in range(d_ff // ff_chunk):
        cs = slice(c * ff_chunk, (c + 1) * ff_chunk)
        wg = wg_ref[:, cs]
        g = jnp.dot(xb, wg, preferred_element_type=F32)
        u = jnp.dot(xb, wu_ref[:, cs], preferred_element_type=F32)
        gh = jnp.dot(xh, wg, preferred_element_type=F32)
        g_before = gh[halo - 1:halo] * has_prev
        g_after = gh[halo:halo + 1] * has_next
        g_m1 = jnp.where(row == 0, g_before, pltpu.roll(g, 1, 0))
        g_p1 = jnp.where(row == tm - 1, g_after, pltpu.roll(g, tm - 1, 0))
        gc = cb_ref[:, cs] + g_m1 * cw_ref[0:1, cs] + g * cw_ref[1:2, cs] + g_p1 * cw_ref[2:3, cs]
        h_sc[:, cs] = (_gelu_tanh(gc) * u).astype(BF16)
    halves = [slice(r, r + tm // 2) for r in (0, tm // 2)]
    ffn = [jnp.dot(h_sc[rows, :], wd_ref[...], preferred_element_type=F32) for rows in halves]
    for rows, f in zip(halves, ffn):
        o_ref[0, rows, :] = _layer_norm(alpha * x_ref[0, rows, :] + f, lg_ref[...], lb_ref[...])


def _ffn(x, wg, wu, cw, cb, wd, lg, lb, *, alpha, tm, ff_chunk, halo=8):
    B, S, D = x.shape
    d_ff = wg.shape[1]
    ns = S // tm
    hb = tm // halo
    last = S // halo - 1
    return pl.pallas_call(
        functools.partial(_ffn_kernel, alpha, ff_chunk),
        grid=(B, ns),
        in_specs=[
            pl.BlockSpec((1, tm, D), lambda b, s: (b, s, 0)),
            pl.BlockSpec((1, halo, D), lambda b, s: (b, jnp.maximum(s * hb - 1, 0), 0)),
            pl.BlockSpec((1, halo, D), lambda b, s: (b, jnp.minimum((s + 1) * hb, last), 0)),
            _resident((D, d_ff)), _resident((D, d_ff)), _resident((CONV_W, d_ff)), _resident((1, d_ff)),
            _resident((d_ff, D)), _resident((1, D)), _resident((1, D)),
        ],
        out_specs=pl.BlockSpec((1, tm, D), lambda b, s: (b, s, 0)),
        out_shape=jax.ShapeDtypeStruct((B, S, D), F32),
        scratch_shapes=[pltpu.VMEM((tm, d_ff), BF16)],
        compiler_params=_cparams(("parallel", "parallel")),
        name="ffn",
    )(x, x, x, wg, wu, cw, cb, wd, lg, lb)


def _rope_tables(seq_len):
    rows_n = seq_len // GRID_W
    row = jnp.repeat(jnp.arange(rows_n, dtype=F32), GRID_W)
    col = jnp.tile(jnp.arange(GRID_W, dtype=F32), rows_n)
    half = HEAD_DIM // 2
    inv_freq = ROPE_THETA ** (-jnp.arange(0, half, 2, dtype=F32) / half)
    ang = jnp.concatenate([row[:, None] * inv_freq, col[:, None] * inv_freq], axis=-1)
    c, s = jnp.cos(ang), jnp.sin(ang)
    return jnp.concatenate([c, c, c, c], axis=-1), jnp.concatenate([-s, s, -s, s], axis=-1)


def _t5_bucket(rel):
    half = N_BUCKETS // 2
    max_exact = half // 2
    bucket = jnp.where(rel > 0, half, 0)
    rp = jnp.abs(rel)
    rpf = jnp.maximum(rp, 1).astype(F32)
    large = max_exact + (jnp.log(rpf / max_exact) / math.log(MAX_DISTANCE / max_exact)
                         * (half - max_exact)).astype(jnp.int32)
    large = jnp.minimum(large, half - 1)
    return bucket + jnp.where(rp < max_exact, rp, large)


def _deinterleave_perm(n_heads):
    per_head = np.concatenate([np.arange(0, HEAD_DIM, 2), np.arange(1, HEAD_DIM, 2)])
    return np.concatenate([h * HEAD_DIM + per_head for h in range(n_heads)])


def kernel(x, rel_bias, w_in, q_norm, k_norm, sink, out_norm_a, out_norm_b, w_out,
           ln1_g, ln1_b, w_gate, w_up, conv_w, conv_b, w_down, ln2_g, ln2_b):
    B, S, D = x.shape
    depth = w_in.shape[0]
    alpha = (2.0 * depth) ** 0.25

    cos_t, sin_t = _rope_tables(S)
    qpos = jnp.arange(W_TQ, dtype=jnp.int32)
    kpos = jnp.arange(W_TK, dtype=jnp.int32) - WINDOW
    bucket = _t5_bucket(kpos[:, None] - qpos[None, :]).astype(jnp.int32)
    blk = np.arange(LANES) // HEAD_DIM
    ones_blk = jnp.asarray(blk[:, None] == blk[None, :], dtype=BF16)

    perm_q = _deinterleave_perm(HEADS_A)
    perm_k = _deinterleave_perm(KV_A)
    col_perm = np.concatenate([perm_q, Q_A + perm_k, np.arange(Q_A + KVD_A, w_in.shape[2])])
    head_perm = _deinterleave_perm(1)

    for l in range(depth):
        w = w_in[l][:, col_perm].astype(BF16)
        gq_max = jnp.max(jnp.abs(q_norm[l])) * (SM_SCALE * LOG2E)
        gk_max = jnp.max(jnp.abs(k_norm[l]))
        bal = jnp.exp2(jnp.round(0.5 * jnp.log2(jnp.maximum(gq_max, 1e-30) / jnp.maximum(gk_max, 1e-30))))
        gq = jnp.tile(q_norm[l][head_perm] * (SM_SCALE * LOG2E) / bal, HEADS_A)[None, :]
        gk = jnp.tile(k_norm[l][head_perm] * bal, KV_A)[None, :]

        def front(use_f8, x, w, gq, gk):
            qat, ka, vat, qbt, kb, vbt = _inproj(x, w, gq, gk, cos_t, sin_t, ones_blk,
                                                 tm=ROW_TILE, tkc=G_CHUNK, tq=G_TQ, use_f8=use_f8)
            return _gattn(qat, ka, vat), qbt, kb, vbt

        bound = math.sqrt(HEAD_DIM) * jnp.maximum(gk_max * bal, gq_max / bal)
        ya, qbt, kb, vbt = lax.cond(bound <= F8_SAFE_MAX, functools.partial(front, True),
                                    functools.partial(front, False), x, w, gq, gk)
        yb = _wattn(rel_bias, sink[l], bucket, qbt, kb, vbt)
        x = _outproj(ya, yb, x, out_norm_a[l][None, :], out_norm_b[l][None, :],
                     w_out[l][:Q_A].astype(BF16), w_out[l][Q_A:].astype(BF16),
                     ln1_g[l][None, :], ln1_b[l][None, :], alpha=alpha, tm=ROW_TILE)
        x = _ffn(x, w_gate[l].astype(BF16), w_up[l].astype(BF16), conv_w[l], conv_b[l][None, :],
                 w_down[l].astype(BF16), ln2_g[l][None, :], ln2_b[l][None, :],
                 alpha=alpha, tm=ROW_TILE, ff_chunk=FF_CHUNK)
    return x
```
